```python
import math
import jax
import jax.numpy as jnp
from jax import lax
import numpy as np

D_MODEL = 4096
BATCH = 2
SEQ = 4096
DEPTH = 1
DEC_BATCH = 128
DEC_SEQ = 4
PAST_LEN = 2048
PAGE_SIZE = 128

HEAD_DIM = 128
MOBA_HEADS = D_MODEL // (2 * HEAD_DIM)
DIFF_HEADS = D_MODEL // (4 * HEAD_DIM)
MOBA_WIDTH = MOBA_HEADS * HEAD_DIM
DIFF_WIDTH = DIFF_HEADS * 2 * HEAD_DIM
MIX_WIDTH = MOBA_WIDTH + DIFF_WIDTH
IN_WIDTH = 3 * MOBA_WIDTH + 3 * DIFF_WIDTH
MOBA_BLOCK = 256
MOBA_TOPK = 3
MOBA_Q_BLOCK = 32
ATTN_Q_BLOCK = 128
ROPE_THETA = 500000.0
ROPE_DIV = 4
N_EXPERTS = 64
N_GROUPS = 8
TOPK_GROUPS = 4
TOP_K = 8
EXPERT_FF = D_MODEL // 4
SHARED_FF = D_MODEL // 4
ROUTED_SCALE = 2.5
MOE_BLOCK = 128
ALPHA = (2.0 * DEPTH) ** 0.25
BETA = (8.0 * DEPTH) ** -0.25
LN_EPS = 1e-5
NEG = -1e30
POOL_EXTRA_DIV = 4

kernel_name = "hymba_moba_diffattn_deepnorm_moe_step"


def layer_norm(x, g, b):
    xf = x.astype(jnp.float32)
    mu = jnp.mean(xf, axis=-1, keepdims=True)
    xc = xf - mu
    var = jnp.mean(xc * xc, axis=-1, keepdims=True)
    y = xc * lax.rsqrt(var + LN_EPS) * g.astype(jnp.float32) + b.astype(jnp.float32)
    return y.astype(x.dtype)


def rms_norm(x, g):
    xf = x.astype(jnp.float32)
    return xf * lax.rsqrt(jnp.mean(xf * xf, axis=-1, keepdims=True) + LN_EPS) * g.astype(jnp.float32)


def rope_partial(x, pos):
    rot = x.shape[-1] // ROPE_DIV
    half = rot // 2
    inv_freq = ROPE_THETA ** (-jnp.arange(half, dtype=jnp.float32) * (2.0 / rot))
    ang = pos.astype(jnp.float32)[:, None] * inv_freq[None, :]
    cos = jnp.cos(ang)[:, None, :]
    sin = jnp.sin(ang)[:, None, :]
    xf = x.astype(jnp.float32)
    x1 = xf[..., :half]
    x2 = xf[..., half:rot]
    out = jnp.concatenate([x1 * cos - x2 * sin, x2 * cos + x1 * sin, xf[..., rot:]], axis=-1)
    return out.astype(x.dtype)


def project_heads(h, w, pos):
    y = jnp.einsum('...sd,de->...se', h, w)
    lead = h.shape[:-1]
    cut = [MOBA_WIDTH, 2 * MOBA_WIDTH, 3 * MOBA_WIDTH,
           3 * MOBA_WIDTH + DIFF_WIDTH, 3 * MOBA_WIDTH + 2 * DIFF_WIDTH]
    qm, km, vm, qd, kd, vd = jnp.split(y, cut, axis=-1)
    qm = rope_partial(qm.reshape(*lead, MOBA_HEADS, HEAD_DIM), pos)
    km = rope_partial(km.reshape(*lead, MOBA_HEADS, HEAD_DIM), pos)
    vm = vm.reshape(*lead, MOBA_HEADS, HEAD_DIM)
    qd = rope_partial(qd.reshape(*lead, 2 * DIFF_HEADS, HEAD_DIM), pos).reshape(*lead, DIFF_HEADS, 2, HEAD_DIM)
    kd = rope_partial(kd.reshape(*lead, 2 * DIFF_HEADS, HEAD_DIM), pos).reshape(*lead, DIFF_HEADS, 2 * HEAD_DIM)
    vd = vd.reshape(*lead, DIFF_HEADS, 2 * HEAD_DIM)
    return qm, km, vm, qd, kd, vd


def moba_blocks(k, v):
    n_len, n_heads, dh = k.shape
    nb = -(-n_len // MOBA_BLOCK)
    pad = nb * MOBA_BLOCK - n_len
    kb = jnp.pad(k, ((0, pad), (0, 0), (0, 0))).reshape(nb, MOBA_BLOCK, n_heads, dh).transpose(2, 0, 1, 3)
    vb = jnp.pad(v, ((0, pad), (0, 0), (0, 0))).reshape(nb, MOBA_BLOCK, n_heads, v.shape[-1]).transpose(2, 0, 1, 3)
    means = jnp.mean(kb.astype(jnp.float32), axis=2)
    nbc = max(nb, MOBA_TOPK)
    means = jnp.pad(means, ((0, 0), (0, nbc - nb), (0, 0)))
    return kb, vb, means


def moba_core(q, pos, kb, vb, means):
    t_len, n_heads, dh = q.shape
    nb = kb.shape[1]
    nbc = means.shape[1]
    qblk = pos // MOBA_BLOCK
    gate = jnp.einsum('thd,hnd->thn', q.astype(jnp.float32), means)
    is_past = jnp.arange(nbc)[None, None, :] < qblk[:, None, None]
    gate = jnp.where(is_past, gate, NEG)
    _, sel = lax.top_k(gate, MOBA_TOPK)
    sel_ok = jnp.arange(MOBA_TOPK)[None, None, :] < jnp.minimum(qblk, MOBA_TOPK)[:, None, None]
    sel = jnp.minimum(sel, nb - 1)
    hidx = jnp.arange(n_heads)
    k_sel = kb[hidx[None, :, None], sel]
    v_sel = vb[hidx[None, :, None], sel]
    k_own = kb[hidx[None, :], qblk[:, None]]
    v_own = vb[hidx[None, :], qblk[:, None]]
    own_pos = qblk[:, None] * MOBA_BLOCK + jnp.arange(MOBA_BLOCK)[None, :]
    own_ok = own_pos <= pos[:, None]
    scale = 1.0 / math.sqrt(dh)
    s_sel = jnp.einsum('thd,thkpd->thkp', q, k_sel).astype(jnp.float32) * scale
    s_sel = jnp.where(sel_ok[..., None], s_sel, NEG).reshape(t_len, n_heads, MOBA_TOPK * MOBA_BLOCK)
    s_own = jnp.einsum('thd,thpd->thp', q, k_own).astype(jnp.float32) * scale
    s_own = jnp.where(own_ok[:, None, :], s_own, NEG)
    p = jax.nn.softmax(jnp.concatenate([s_sel, s_own], axis=-1), axis=-1)
    p_sel = p[..., :MOBA_TOPK * MOBA_BLOCK].reshape(t_len, n_heads, MOBA_TOPK, MOBA_BLOCK).astype(vb.dtype)
    p_own = p[..., MOBA_TOPK * MOBA_BLOCK:].astype(vb.dtype)
    o = jnp.einsum('thkp,thkpd->thd', p_sel, v_sel) + jnp.einsum('thp,thpd->thd', p_own, v_own)
    return o.astype(q.dtype)


def diff_core(q, k, v, mask, lam, g, lam_init):
    scale = 1.0 / math.sqrt(q.shape[-1])
    s = jnp.einsum('...thcd,...lhcd->...hctl', q, k).astype(jnp.float32) * scale
    s = jnp.where(mask, s, NEG)
    a = jax.nn.softmax(s, axis=-1)
    w = a[..., 0, :, :] - lam * a[..., 1, :, :]
    o = jnp.einsum('...htl,...lhe->...the', w.astype(v.dtype), v)
    return (rms_norm(o, g) * (1.0 - lam_init)).astype(v.dtype)


def mix_prompt(h, l, w_in, lam, g, lam_init, w_out):
    bsz, s_len, _ = h.shape
    pos = jnp.arange(s_len, dtype=jnp.int32)
    qm, km, vm, qd, kd, vd = project_heads(h, w_in[l], pos)

    def moba_seq(args):
        q_s, k_s, v_s = args
        kb, vb, means = moba_blocks(k_s, v_s)
        nq = s_len // MOBA_Q_BLOCK
        qc = q_s.reshape(nq, MOBA_Q_BLOCK, MOBA_HEADS, HEAD_DIM)
        pc = pos.reshape(nq, MOBA_Q_BLOCK)
        o = lax.map(lambda a: moba_core(a[0], a[1], kb, vb, means), (qc, pc))
        return o.reshape(s_len, MOBA_HEADS, HEAD_DIM)

    om = lax.map(moba_seq, (qm, km, vm))

    nqb = s_len // ATTN_Q_BLOCK
    kd4 = kd.reshape(bsz, s_len, DIFF_HEADS, 2, HEAD_DIM)
    qdb = jnp.moveaxis(qd.reshape(bsz, nqb, ATTN_Q_BLOCK, DIFF_HEADS, 2, HEAD_DIM), 1, 0)
    pb = pos.reshape(nqb, ATTN_Q_BLOCK)

    def diff_blk(a):
        q_b, p_b = a
        mask = pos[None, :] <= p_b[:, None]
        return diff_core(q_b, kd4, vd, mask, lam, g, lam_init)

    od = jnp.moveaxis(lax.map(diff_blk, (qdb, pb)), 0, 1).reshape(bsz, s_len, DIFF_WIDTH)
    o = jnp.concatenate([om.reshape(bsz, s_len, MOBA_WIDTH), od.astype(om.dtype)], axis=-1)
    return o @ w_out[l], km, vm, kd, vd


def mix_sample(h, l, cache_mk, cache_mv, cache_dk, cache_dv, page_table, w_in, lam, g, lam_init, w_out):
    dbsz, t_len, _ = h.shape
    past = page_table.shape[1] * cache_mk.shape[2]
    pos = past + jnp.arange(t_len, dtype=jnp.int32)
    qm, km, vm, qd, kd, vd = project_heads(h, w_in[l], pos)
    n_keys = past + t_len
    key_pos = jnp.arange(n_keys, dtype=jnp.int32)

    def gather(pool, pages):
        rows = pool[l, pages]
        return rows.reshape(past, *pool.shape[3:])

    def seq(a):
        qm_s, km_s, vm_s, qd_s, kd_s, vd_s, pages = a
        k_all = jnp.concatenate([gather(cache_mk, pages), km_s.astype(cache_mk.dtype)], axis=0)
        v_all = jnp.concatenate([gather(cache_mv, pages), vm_s.astype(cache_mv.dtype)], axis=0)
        kb, vb, means = moba_blocks(k_all, v_all)
        om_s = moba_core(qm_s, pos, kb, vb, means)
        kd_all = jnp.concatenate([gather(cache_dk, pages), kd_s.astype(cache_dk.dtype)], axis=0)
        kd_all = kd_all.reshape(n_keys, DIFF_HEADS, 2, HEAD_DIM)
        vd_all = jnp.concatenate([gather(cache_dv, pages), vd_s.astype(cache_dv.dtype)], axis=0)
        mask = key_pos[None, :] <= pos[:, None]
        od_s = diff_core(qd_s, kd_all, vd_all, mask, lam, g, lam_init)
        return om_s, od_s

    om, od = lax.map(seq, (qm, km, vm, qd, kd, vd, page_table))
    o = jnp.concatenate([om.reshape(dbsz, t_len, MOBA_WIDTH),
                         od.reshape(dbsz, t_len, DIFF_WIDTH).astype(om.dtype)], axis=-1)
    return o @ w_out[l], km, vm, kd, vd


def swiglu(x, wg, wu, wd):
    return (jax.nn.silu(x @ wg) * (x @ wu)) @ wd


def routed_experts(t, sel, gate, w_g, w_u, w_d, l):
    n_tok, d = t.shape
    n_asg = n_tok * TOP_K
    flat_e = sel.reshape(n_asg)
    flat_tok = jnp.repeat(jnp.arange(n_tok, dtype=jnp.int32), TOP_K)
    flat_gate = gate.reshape(n_asg)
    order = jnp.argsort(flat_e)
    e_sorted = flat_e[order]
    counts = jnp.bincount(flat_e, length=N_EXPERTS)
    padded = (counts + MOE_BLOCK - 1) // MOE_BLOCK * MOE_BLOCK
    pad_end = jnp.cumsum(padded)
    pad_start = pad_end - padded
    seg_start = jnp.cumsum(counts) - counts
    dest = pad_start[e_sorted] + jnp.arange(n_asg) - seg_start[e_sorted]
    n_blocks = -(-n_asg // MOE_BLOCK) + N_EXPERTS
    n_slots = n_blocks * MOE_BLOCK
    slot_tok = jnp.full((n_slots,), n_tok, jnp.int32).at[dest].set(flat_tok[order])
    slot_gate = jnp.zeros((n_slots,), t.dtype).at[dest].set(flat_gate[order])
    block_start = jnp.arange(n_blocks, dtype=pad_end.dtype) * MOE_BLOCK
    block_expert = jnp.minimum(jnp.searchsorted(pad_end, block_start, side='right'), N_EXPERTS - 1)
    t_pad = jnp.concatenate([t, jnp.zeros((1, d), t.dtype)], axis=0)

    def block_fn(args):
        tok, gt, e = args
        xb = t_pad[tok]
        hb = jax.nn.silu(xb @ w_g[l, e]) * (xb @ w_u[l, e])
        return (hb @ w_d[l, e]) * gt[:, None]

    yb = lax.map(block_fn, (slot_tok.reshape(n_blocks, MOE_BLOCK),
                            slot_gate.reshape(n_blocks, MOE_BLOCK), block_expert))
    out = jnp.zeros((n_tok + 1, d), t.dtype).at[slot_tok].add(yb.reshape(n_slots, d))
    return out[:n_tok]


def moe_ffn(h, l, w_router, b_router, w_exp_gate, w_exp_up, w_exp_down, w_sh_gate, w_sh_up, w_sh_down):
    shape = h.shape
    t = h.reshape(-1, shape[-1])
    n_tok = t.shape[0]
    scores = jax.nn.sigmoid(jnp.einsum('td,de->te', t.astype(jnp.float32), w_router[l].astype(jnp.float32)))
    choice = scores + b_router[l].astype(jnp.float32)
    per_group = N_EXPERTS // N_GROUPS
    grp_score = jnp.sum(lax.top_k(choice.reshape(n_tok, N_GROUPS, per_group), 2)[0], axis=-1)
    _, grp_idx = lax.top_k(grp_score, TOPK_GROUPS)
    grp_keep = jnp.any(grp_idx[:, :, None] == jnp.arange(N_GROUPS)[None, None, :], axis=1)
    keep = jnp.repeat(grp_keep, per_group, axis=-1)
    _, sel = lax.top_k(jnp.where(keep, choice, NEG), TOP_K)
    wsel = jnp.take_along_axis(scores, sel, axis=-1)
    wsel = wsel / jnp.sum(wsel, axis=-1, keepdims=True) * ROUTED_SCALE
    routed = routed_experts(t, sel, wsel.astype(t.dtype), w_exp_gate, w_exp_up, w_exp_down, l)
    shared = swiglu(t, w_sh_gate[l], w_sh_up[l], w_sh_down[l])
    return (routed + shared).reshape(shape)


def setup_inputs(seed: int = 0) -> dict:
    key = jax.random.key(seed)
    ks = jax.random.split(key, 32)
    f32 = jnp.float32
    n_pages = PAST_LEN // PAGE_SIZE
    n_used = DEC_BATCH * n_pages
    n_pool = n_used + n_used // POOL_EXTRA_DIV

    def nrm(k, shape, scale):
        return jax.random.normal(k, shape, f32) * scale

    x_prompt = jax.random.normal(ks[0], (BATCH, SEQ, D_MODEL), f32)
    x_sample = jax.random.normal(ks[1], (DEC_BATCH, DEC_SEQ, D_MODEL), f32)
    cache_moba_k = jax.random.normal(ks[2], (DEPTH, n_pool, PAGE_SIZE, MOBA_HEADS, HEAD_DIM), f32)
    cache_moba_v = jax.random.normal(ks[3], (DEPTH, n_pool, PAGE_SIZE, MOBA_HEADS, HEAD_DIM), f32)
    cache_diff_k = jax.random.normal(ks[4], (DEPTH, n_pool, PAGE_SIZE, DIFF_HEADS, 2 * HEAD_DIM), f32)
    cache_diff_v = jax.random.normal(ks[5], (DEPTH, n_pool, PAGE_SIZE, DIFF_HEADS, 2 * HEAD_DIM), f32)
    page_table = jax.random.permutation(ks[6], n_pool)[:n_used].reshape(DEC_BATCH, n_pages).astype(jnp.int32)
    ln_in_g = 1.0 + nrm(ks[7], (D_MODEL,), 0.02)
    ln_in_b = nrm(ks[8], (D_MODEL,), 0.02)
    col_scale = jnp.concatenate([jnp.ones((2 * MOBA_WIDTH,), f32), jnp.full((MOBA_WIDTH,), BETA, f32),
                                 jnp.ones((2 * DIFF_WIDTH,), f32), jnp.full((DIFF_WIDTH,), BETA, f32)])
    w_in = nrm(ks[9], (DEPTH, D_MODEL, IN_WIDTH), D_MODEL ** -0.5) * col_scale
    diff_lq1 = nrm(ks[10], (DEPTH, HEAD_DIM), 0.1)
    diff_lk1 = nrm(ks[11], (DEPTH, HEAD_DIM), 0.1)
    diff_lq2 = nrm(ks[12], (DEPTH, HEAD_DIM), 0.1)
    diff_lk2 = nrm(ks[13], (DEPTH, HEAD_DIM), 0.1)
    diff_norm_g = 1.0 + nrm(ks[14], (DEPTH, 2 * HEAD_DIM), 0.02)
    w_out = nrm(ks[15], (DEPTH, MIX_WIDTH, D_MODEL), MIX_WIDTH ** -0.5 * BETA)
    ln1_g = 1.0 + nrm(ks[16], (DEPTH, D_MODEL), 0.02)
    ln1_b = nrm(ks[17], (DEPTH, D_MODEL), 0.02)
    w_router = nrm(ks[18], (DEPTH, D_MODEL, N_EXPERTS), D_MODEL ** -0.5)
    b_router = nrm(ks[19], (DEPTH, N_EXPERTS), 0.01)
    w_exp_gate = nrm(ks[20], (DEPTH, N_EXPERTS, D_MODEL, EXPERT_FF), D_MODEL ** -0.5 * BETA)
    w_exp_up = nrm(ks[21], (DEPTH, N_EXPERTS, D_MODEL, EXPERT_FF), D_MODEL ** -0.5 * BETA)
    w_exp_down = nrm(ks[22], (DEPTH, N_EXPERTS, EXPERT_FF, D_MODEL), EXPERT_FF ** -0.5 * BETA)
    w_sh_gate = nrm(ks[23], (DEPTH, D_MODEL, SHARED_FF), D_MODEL ** -0.5 * BETA)
    w_sh_up = nrm(ks[24], (DEPTH, D_MODEL, SHARED_FF), D_MODEL ** -0.5 * BETA)
    w_sh_down = nrm(ks[25], (DEPTH, SHARED_FF, D_MODEL), SHARED_FF ** -0.5 * BETA)
    ln2_g = 1.0 + nrm(ks[26], (DEPTH, D_MODEL), 0.02)
    ln2_b = nrm(ks[27], (DEPTH, D_MODEL), 0.02)
    return {"x_prompt": x_prompt, "x_sample": x_sample,
            "cache_moba_k": cache_moba_k, "cache_moba_v": cache_moba_v,
            "cache_diff_k": cache_diff_k, "cache_diff_v": cache_diff_v,
            "page_table": page_table, "ln_in_g": ln_in_g, "ln_in_b": ln_in_b,
            "w_in": w_in, "diff_lq1": diff_lq1, "diff_lk1": diff_lk1,
            "diff_lq2": diff_lq2, "diff_lk2": diff_lk2, "diff_norm_g": diff_norm_g,
            "w_out": w_out, "ln1_g": ln1_g, "ln1_b": ln1_b,
            "w_router": w_router, "b_router": b_router,
            "w_exp_gate": w_exp_gate, "w_exp_up": w_exp_up, "w_exp_down": w_exp_down,
            "w_sh_gate": w_sh_gate, "w_sh_up": w_sh_up, "w_sh_down": w_sh_down,
            "ln2_g": ln2_g, "ln2_b": ln2_b}


def reference(x_prompt, x_sample, cache_moba_k, cache_moba_v, cache_diff_k, cache_diff_v, page_table,
              ln_in_g, ln_in_b, w_in, diff_lq1, diff_lk1, diff_lq2, diff_lk2, diff_norm_g, w_out,
              ln1_g, ln1_b, w_router, b_router, w_exp_gate, w_exp_up, w_exp_down,
              w_sh_gate, w_sh_up, w_sh_down, ln2_g, ln2_b):
    hp = layer_norm(x_prompt, ln_in_g, ln_in_b)
    hs = layer_norm(x_sample, ln_in_g, ln_in_b)
    mk_p, mv_p, dk_p, dv_p = [], [], [], []
    mk_s, mv_s, dk_s, dv_s = [], [], [], []
    for l in range(DEPTH):
        lam_init = 0.8 - 0.6 * math.exp(-0.3 * l)
        lam = (jnp.exp(jnp.sum(diff_lq1[l].astype(jnp.float32) * diff_lk1[l].astype(jnp.float32)))
               - jnp.exp(jnp.sum(diff_lq2[l].astype(jnp.float32) * diff_lk2[l].astype(jnp.float32)))
               + lam_init)
        g = diff_norm_g[l]
        a_p, k1, v1, k2, v2 = mix_prompt(hp, l, w_in, lam, g, lam_init, w_out)
        mk_p.append(k1); mv_p.append(v1); dk_p.append(k2); dv_p.append(v2)
        a_s, k1, v1, k2, v2 = mix_sample(hs, l, cache_moba_k, cache_moba_v, cache_diff_k, cache_diff_v,
                                         page_table, w_in, lam, g, lam_init, w_out)
        mk_s.append(k1); mv_s.append(v1); dk_s.append(k2); dv_s.append(v2)
        hp = layer_norm(ALPHA * hp + a_p, ln1_g[l], ln1_b[l])
        hs = layer_norm(ALPHA * hs + a_s, ln1_g[l], ln1_b[l])
        f_p = moe_ffn(hp, l, w_router, b_router, w_exp_gate, w_exp_up, w_exp_down, w_sh_gate, w_sh_up, w_sh_down)
        f_s = moe_ffn(hs, l, w_router, b_router, w_exp_gate, w_exp_up, w_exp_down, w_sh_gate, w_sh_up, w_sh_down)
        hp = layer_norm(ALPHA * hp + f_p, ln2_g[l], ln2_b[l])
        hs = layer_norm(ALPHA * hs + f_s, ln2_g[l], ln2_b[l])
    return (hp, hs, jnp.stack(mk_p), jnp.stack(mv_p), jnp.stack(dk_p), jnp.stack(dv_p),
            jnp.stack(mk_s), jnp.stack(mv_s), jnp.stack(dk_s), jnp.stack(dv_s))
```

```python
import functools
import math

import jax
import jax.numpy as jnp
from jax import lax
from jax.experimental import pallas as pl
from jax.experimental.pallas import tpu as pltpu

F32 = jnp.float32
BF16 = jnp.bfloat16

HEAD_DIM = 128
MOBA_BLOCK = 256
MOBA_TOPK = 3
ROPE_THETA = 500000.0
ROPE_DIV = 4
N_GROUPS = 8
TOPK_GROUPS = 4
TOP_K = 8
ROUTED_SCALE = 2.5
LN_EPS = 1e-5
NEG = -1e30
DEPTH = 1
ALPHA = (2.0 * DEPTH) ** 0.25
LAM_INIT = 0.8 - 0.6 * math.exp(-0.3 * 0)

VMEM_LIMIT_BYTES = 56 * 1024 * 1024
LANES = 128

ROW_TILE = 256
MM_TM = 512
MM_TN = 512
ATT_TQ = MOBA_BLOCK
MOE_TM = 256
MOE_TF = 512
COMBINE_TT = 32


def _cparams(*sem):
    return pltpu.CompilerParams(dimension_semantics=sem, vmem_limit_bytes=VMEM_LIMIT_BYTES)


def _dot(a, b):
    return jnp.dot(a, b, preferred_element_type=F32)


def _dot_nt(a, b):
    return lax.dot_general(a, b, (((1,), (1,)), ((), ())), preferred_element_type=F32)


def _layer_norm(x, g, b):
    mu = jnp.mean(x, axis=-1, keepdims=True)
    xc = x - mu
    var = jnp.mean(xc * xc, axis=-1, keepdims=True)
    return xc * lax.rsqrt(var + LN_EPS) * g + b


def _sigmoid(x):
    return 1.0 / (1.0 + jnp.exp(-x))


def _ln_in_body(x_ref, g_ref, b_ref, o_ref):
    o_ref[...] = _layer_norm(x_ref[...], g_ref[...], b_ref[...]).astype(o_ref.dtype)


def _ln_in(x, g, b):
    n, d = x.shape
    row = pl.BlockSpec((ROW_TILE, d), lambda i: (i, 0))
    vec = pl.BlockSpec((1, d), lambda i: (0, 0))
    return pl.pallas_call(
        _ln_in_body, grid=(n // ROW_TILE,), in_specs=[row, vec, vec], out_specs=row,
        out_shape=jax.ShapeDtypeStruct((n, d), BF16), compiler_params=_cparams("parallel"),
        name="ln_in")(x, g.reshape(1, d), b.reshape(1, d))


def _rope_chunk(y, c, sa, sb):
    return y * c + pltpu.roll(y, HEAD_DIM - 16, 1) * sa + pltpu.roll(y, 16, 1) * sb


def _proj_body(*refs, rope, scale, n_out):
    if rope:
        h_ref, w_ref, c_ref, sa_ref, sb_ref = refs[:5]
    else:
        h_ref, w_ref = refs[:2]
    out_refs = refs[len(refs) - n_out:]
    acc = _dot(h_ref[...], w_ref[...])
    if rope:
        c, sa, sb = c_ref[...], sa_ref[...], sb_ref[...]
        acc = jnp.concatenate(
            [_rope_chunk(acc[:, k * HEAD_DIM:(k + 1) * HEAD_DIM], c, sa, sb)
             for k in range(acc.shape[1] // HEAD_DIM)], axis=1)
    if scale != 1.0:
        acc = acc * scale
    for o_ref in out_refs:
        o_ref[...] = acc.astype(o_ref.dtype)


def _proj(h, w, col0, ncols, out_dtypes, rope_tabs=None, scale=1.0, name="proj"):
    n, k = h.shape
    tm, tn = MM_TM, MM_TN
    cb = col0 // tn
    in_specs = [pl.BlockSpec((tm, k), lambda i, j: (i, 0)),
                pl.BlockSpec((k, tn), lambda i, j: (0, cb + j))]
    args = [h, w]
    if rope_tabs is not None:
        in_specs += [pl.BlockSpec((tm, HEAD_DIM), lambda i, j: (i, 0))] * 3
        args += list(rope_tabs)
    out_spec = pl.BlockSpec((tm, tn), lambda i, j: (i, j))
    outs = pl.pallas_call(
        functools.partial(_proj_body, rope=rope_tabs is not None, scale=scale, n_out=len(out_dtypes)),
        grid=(n // tm, ncols // tn), in_specs=in_specs, out_specs=[out_spec] * len(out_dtypes),
        out_shape=[jax.ShapeDtypeStruct((n, ncols), dt) for dt in out_dtypes],
        compiler_params=_cparams("parallel", "arbitrary"), name=name)(*args)
    return outs


def _rope_tables(pos):
    rot = HEAD_DIM // ROPE_DIV
    half = rot // 2
    inv_freq = ROPE_THETA ** (-jnp.arange(half, dtype=F32) * (2.0 / rot))
    ang = pos.astype(F32)[:, None] * inv_freq[None, :]
    cos, sin = jnp.cos(ang), jnp.sin(ang)
    n = pos.shape[0]
    ones = jnp.ones((n, HEAD_DIM - rot), F32)
    zeros = jnp.zeros((n, HEAD_DIM - rot), F32)
    zh = jnp.zeros((n, half), F32)
    c = jnp.concatenate([cos, cos, ones], axis=1)
    sa = jnp.concatenate([-sin, zh, zeros], axis=1)
    sb = jnp.concatenate([zh, sin, zeros], axis=1)
    return c, sa, sb


def _means_body(k_ref, o_ref):
    o_ref[0] = jnp.mean(k_ref[...], axis=0, keepdims=True)


def _moba_means(k, n_rows):
    w = k.shape[1]
    nb = n_rows // MOBA_BLOCK
    return pl.pallas_call(
        _means_body, grid=(nb,), in_specs=[pl.BlockSpec((MOBA_BLOCK, w), lambda i: (i, 0))],
        out_specs=pl.BlockSpec((1, 1, w), lambda i: (i, 0, 0)),
        out_shape=jax.ShapeDtypeStruct((nb, 1, w), F32), compiler_params=_cparams("parallel"),
        name="moba_means")(k)


def _pick_top(cur, idx_f, n_pick, axis, big):
    sel = jnp.zeros(cur.shape, jnp.bool_)
    for _ in range(n_pick):
        mx = jnp.max(cur, axis=axis, keepdims=True)
        first = jnp.min(jnp.where(cur == mx, idx_f, big), axis=axis, keepdims=True)
        pick = idx_f == first
        sel = jnp.logical_or(sel, pick)
        cur = jnp.where(pick, -jnp.inf, cur)
    return sel


def _moba_prompt_body(q_ref, k_ref, v_ref, mean_ref, o_ref):
    qi = pl.program_id(2)
    tq = q_ref.shape[0]
    q = q_ref[...]
    mean = mean_ref[0]
    m_hi = mean.astype(BF16)
    m_lo = (mean - m_hi.astype(F32)).astype(BF16)
    gate = _dot_nt(q, m_hi) + _dot_nt(q, m_lo)
    lane = lax.broadcasted_iota(jnp.int32, gate.shape, 1)
    lane_f = lane.astype(F32)
    past = lane < qi
    sel = _pick_top(jnp.where(past, gate, -jnp.inf), lane_f, MOBA_TOPK, 1, float(LANES))
    sel_f = jnp.where(jnp.logical_and(sel, past), 1.0, 0.0)

    row = lax.broadcasted_iota(jnp.int32, (tq, tq), 0)
    col = lax.broadcasted_iota(jnp.int32, (tq, tq), 1)
    own = pl.ds(pl.multiple_of(qi * tq, tq), tq)
    s = jnp.where(col <= row, _dot_nt(q, k_ref[own, :]), NEG)
    m = jnp.max(s, axis=1, keepdims=True)
    p = jnp.exp(s - m)
    l = jnp.sum(p, axis=1, keepdims=True)
    acc = _dot(p.astype(BF16), v_ref[own, :])

    def body(ki, carry):
        m, l, acc = carry
        blk = pl.ds(pl.multiple_of(ki * tq, tq), tq)
        picked = jnp.sum(jnp.where(lane == ki, sel_f, 0.0), axis=1, keepdims=True) > 0.5
        s = jnp.where(picked, _dot_nt(q, k_ref[blk, :]), NEG)
        m_new = jnp.maximum(m, jnp.max(s, axis=1, keepdims=True))
        alpha = jnp.exp(m - m_new)
        p = jnp.exp(s - m_new)
        l = alpha * l + jnp.sum(p, axis=1, keepdims=True)
        acc = alpha * acc + _dot(p.astype(BF16), v_ref[blk, :])
        return m_new, l, acc

    m, l, acc = lax.fori_loop(0, qi, body, (m, l, acc))
    o_ref[...] = (acc / l).astype(o_ref.dtype)


def _moba_prompt(q, k, v, means, bsz, s_len):
    heads = q.shape[1] // HEAD_DIM
    nq = s_len // ATT_TQ
    kv_spec = pl.BlockSpec((s_len, HEAD_DIM), lambda b, h, i: (b, h))
    return pl.pallas_call(
        _moba_prompt_body, grid=(bsz, heads, nq),
        in_specs=[pl.BlockSpec((ATT_TQ, HEAD_DIM), lambda b, h, i: (b * nq + i, h)), kv_spec, kv_spec,
                  pl.BlockSpec((1, LANES, HEAD_DIM), lambda b, h, i: (b, 0, h))],
        out_specs=pl.BlockSpec((ATT_TQ, HEAD_DIM), lambda b, h, i: (b * nq + i, h)),
        out_shape=jax.ShapeDtypeStruct((bsz * s_len, q.shape[1]), BF16),
        compiler_params=_cparams("parallel", "parallel", "arbitrary"), name="moba_prompt")(q, k, v, means)


def _lambda(lq1, lk1, lq2, lk2):
    return (jnp.exp(jnp.sum(lq1 * lk1, axis=1, keepdims=True))
            - jnp.exp(jnp.sum(lq2 * lk2, axis=1, keepdims=True)) + LAM_INIT)


def _diff_out(acc0, l0, acc1, l1, lam, g):
    o = acc0 / l0 - lam * (acc1 / l1)
    ms = jnp.mean(o * o, axis=1, keepdims=True)
    return o * lax.rsqrt(ms + LN_EPS) * g * (1.0 - LAM_INIT)


def _diff_prompt_body(q_ref, k_ref, v_ref, g_ref, lq1, lk1, lq2, lk2, o_ref, m_ref, l_ref, acc_ref):
    qi = pl.program_id(2)
    tq = q_ref.shape[0]
    q = q_ref[...]
    row = lax.broadcasted_iota(jnp.int32, (tq, tq), 0)
    col = lax.broadcasted_iota(jnp.int32, (tq, tq), 1)
    own = pl.ds(pl.multiple_of(qi * tq, tq), tq)
    k_own = k_ref[own, :]
    v_own = v_ref[own, :]
    for c in range(2):
        cs = slice(c * HEAD_DIM, (c + 1) * HEAD_DIM)
        s = jnp.where(col <= row, _dot_nt(q[:, cs], k_own[:, cs]), NEG)
        m = jnp.max(s, axis=1, keepdims=True)
        p = jnp.exp(s - m)
        m_ref[c] = m
        l_ref[c] = jnp.sum(p, axis=1, keepdims=True)
        acc_ref[c] = _dot(p.astype(BF16), v_own)

    def body(ki, carry):
        blk = pl.ds(pl.multiple_of(ki * tq, tq), tq)
        kb = k_ref[blk, :]
        vb = v_ref[blk, :]
        for c in range(2):
            cs = slice(c * HEAD_DIM, (c + 1) * HEAD_DIM)
            s = _dot_nt(q[:, cs], kb[:, cs])
            m_old = m_ref[c]
            m_new = jnp.maximum(m_old, jnp.max(s, axis=1, keepdims=True))
            alpha = jnp.exp(m_old - m_new)
            p = jnp.exp(s - m_new)
            l_ref[c] = alpha * l_ref[c] + jnp.sum(p, axis=1, keepdims=True)
            acc_ref[c] = alpha * acc_ref[c] + _dot(p.astype(BF16), vb)
            m_ref[c] = m_new
        return carry

    lax.fori_loop(0, qi, body, 0)
    lam = _lambda(lq1[...], lk1[...], lq2[...], lk2[...])
    o_ref[...] = _diff_out(acc_ref[0], l_ref[0], acc_ref[1], l_ref[1], lam, g_ref[...]).astype(o_ref.dtype)


def _diff_prompt(q, k, v, g, lams, bsz, s_len):
    hw = 2 * HEAD_DIM
    heads = q.shape[1] // hw
    nq = s_len // ATT_TQ
    kv_spec = pl.BlockSpec((s_len, hw), lambda b, h, i: (b, h))
    vec = pl.BlockSpec((1, HEAD_DIM), lambda b, h, i: (0, 0))
    return pl.pallas_call(
        _diff_prompt_body, grid=(bsz, heads, nq),
        in_specs=[pl.BlockSpec((ATT_TQ, hw), lambda b, h, i: (b * nq + i, h)), kv_spec, kv_spec,
                  pl.BlockSpec((1, hw), lambda b, h, i: (0, 0)), vec, vec, vec, vec],
        out_specs=pl.BlockSpec((ATT_TQ, hw), lambda b, h, i: (b * nq + i, h)),
        out_shape=jax.ShapeDtypeStruct((bsz * s_len, q.shape[1]), BF16),
        scratch_shapes=[pltpu.VMEM((2, ATT_TQ, 1), F32), pltpu.VMEM((2, ATT_TQ, 1), F32),
                        pltpu.VMEM((2, ATT_TQ, hw), F32)],
        compiler_params=_cparams("parallel", "parallel", "arbitrary"), name="diff_prompt")(q, k, v, g, *lams)


def _sample_attn_body(pt_ref, qm_ref, qd_ref, kmn_ref, vmn_ref, kdn_ref, vdn_ref,
                      mk0, mk1, mv0, mv1, dk0, dk1, dv0, dv1, g_ref, lq1, lk1, lq2, lk2,
                      om_ref, od_ref, gate_ref, mall_ref, lall_ref, oblk_ref, md_ref, ld_ref, accd_ref,
                      *, t_len):
    del pt_ref
    n = pl.program_id(1)
    n_blk = pl.num_programs(1)
    page, m_heads, _ = mk0.shape[2:]
    d_heads, hw = dk0.shape[3:]
    qm = qm_ref[0]
    qd = qd_ref[0]
    rm, rd = qm.shape[0], qd.shape[0]
    lane = lax.broadcasted_iota(jnp.int32, (rm, LANES), 1)

    def head_match(rows, cols, heads):
        r = lax.broadcasted_iota(jnp.int32, (rows, cols), 0)
        c = lax.broadcasted_iota(jnp.int32, (rows, cols), 1)
        return r, c, (r % heads) == (c % heads)

    def rows2d(ref):
        x = ref[0, 0]
        return x.reshape(x.shape[0] * x.shape[1], x.shape[2])

    @pl.when(n == 0)
    def _init():
        gate_ref[...] = jnp.zeros(gate_ref.shape, F32)
        mall_ref[...] = jnp.zeros(mall_ref.shape, F32)
        lall_ref[...] = jnp.zeros(lall_ref.shape, F32)
        md_ref[...] = jnp.full(md_ref.shape, NEG, F32)
        ld_ref[...] = jnp.zeros(ld_ref.shape, F32)
        accd_ref[...] = jnp.zeros(accd_ref.shape, F32)

    k0, k1 = mk0[0, 0], mk1[0, 0]
    mean = (jnp.sum(k0, axis=0) + jnp.sum(k1, axis=0)) * (1.0 / (2 * page))
    mean_rows = jnp.concatenate([mean] * t_len, axis=0)
    gate_n = jnp.sum(qm.astype(F32) * mean_rows, axis=1, keepdims=True)
    kb = jnp.concatenate([rows2d(mk0), rows2d(mk1)], axis=0).astype(BF16)
    vb = jnp.concatenate([rows2d(mv0), rows2d(mv1)], axis=0).astype(BF16)
    _, _, same = head_match(rm, kb.shape[0], m_heads)
    s = jnp.where(same, _dot_nt(qm, kb), NEG)
    m_n = jnp.max(s, axis=1, keepdims=True)
    p = jnp.exp(s - m_n)
    l_n = jnp.sum(p, axis=1, keepdims=True)
    oblk_ref[n] = _dot(p.astype(BF16), vb)
    here = lane == n
    gate_ref[...] = jnp.where(here, gate_n, gate_ref[...])
    mall_ref[...] = jnp.where(here, m_n, mall_ref[...])
    lall_ref[...] = jnp.where(here, l_n, lall_ref[...])

    kd = jnp.concatenate([rows2d(dk0), rows2d(dk1)], axis=0).astype(BF16)
    vd = jnp.concatenate([rows2d(dv0), rows2d(dv1)], axis=0).astype(BF16)
    _, _, same_d = head_match(rd, kd.shape[0], d_heads)

    def diff_update(c, s, v):
        m_old = md_ref[c]
        m_new = jnp.maximum(m_old, jnp.max(s, axis=1, keepdims=True))
        alpha = jnp.exp(m_old - m_new)
        p = jnp.exp(s - m_new)
        ld_ref[c] = alpha * ld_ref[c] + jnp.sum(p, axis=1, keepdims=True)
        accd_ref[c] = alpha * accd_ref[c] + _dot(p.astype(BF16), v)
        md_ref[c] = m_new

    for c in range(2):
        cs = slice(c * HEAD_DIM, (c + 1) * HEAD_DIM)
        diff_update(c, jnp.where(same_d, _dot_nt(qd[:, cs], kd[:, cs]), NEG), vd)

    @pl.when(n == n_blk - 1)
    def _finish():
        r, c, same = head_match(rm, rm, m_heads)
        own_ok = jnp.logical_and(same, c // m_heads <= r // m_heads)
        s_own = jnp.where(own_ok, _dot_nt(qm, kmn_ref[0].astype(BF16)), NEG)
        cached = lane < n_blk
        sel = _pick_top(jnp.where(cached, gate_ref[...], -jnp.inf), lane.astype(F32), MOBA_TOPK, 1,
                        float(LANES))
        sel = jnp.logical_and(sel, cached)
        mall = mall_ref[...]
        m_tot = jnp.maximum(jnp.max(jnp.where(sel, mall, NEG), axis=1, keepdims=True),
                            jnp.max(s_own, axis=1, keepdims=True))
        w_blk = jnp.where(sel, jnp.exp(mall - m_tot), 0.0)
        p_own = jnp.exp(s_own - m_tot)
        denom = (jnp.sum(w_blk * lall_ref[...], axis=1, keepdims=True)
                 + jnp.sum(p_own, axis=1, keepdims=True))
        o_m = _dot(p_own.astype(BF16), vmn_ref[0].astype(BF16))
        for b in range(oblk_ref.shape[0]):
            o_m = o_m + w_blk[:, b:b + 1] * oblk_ref[b]
        om_ref[0] = o_m / denom

        r, c, same = head_match(rd, rd, d_heads)
        own_ok = jnp.logical_and(same, c // d_heads <= r // d_heads)
        kdn = kdn_ref[0].astype(BF16)
        vdn = vdn_ref[0].astype(BF16)
        for comp in range(2):
            cs = slice(comp * HEAD_DIM, (comp + 1) * HEAD_DIM)
            diff_update(comp, jnp.where(own_ok, _dot_nt(qd[:, cs], kdn[:, cs]), NEG), vdn)
        lam = _lambda(lq1[...], lk1[...], lq2[...], lk2[...])
        od_ref[0] = _diff_out(accd_ref[0], ld_ref[0], accd_ref[1], ld_ref[1], lam, g_ref[...])


def _sample_attn(page_table, qm, qd, kmn, vmn, kdn, vdn, cmk, cmv, cdk, cdv, g, lams, lyr, t_len):
    dbsz, rm, _ = qm.shape
    rd, hw = qd.shape[1:]
    n_pages = page_table.shape[1]
    page, m_heads = cmk.shape[2:4]
    d_heads = cdk.shape[3]
    ppb = MOBA_BLOCK // page
    assert ppb == 2 and (n_pages * page) % MOBA_BLOCK == 0 and t_len <= MOBA_BLOCK
    n_blk = n_pages // ppb
    assert MOBA_TOPK <= n_blk <= LANES

    def tok_spec(rows, w):
        return pl.BlockSpec((1, rows, w), lambda s, n, pt: (s, 0, 0))

    def page_spec(j, heads, w):
        return pl.BlockSpec((1, 1, page, heads, w),
                            lambda s, n, pt: (lyr, pt[s * n_pages + ppb * n + j], 0, 0, 0))

    mp = [page_spec(0, m_heads, HEAD_DIM), page_spec(1, m_heads, HEAD_DIM)]
    dp = [page_spec(0, d_heads, hw), page_spec(1, d_heads, hw)]
    vec = pl.BlockSpec((1, HEAD_DIM), lambda s, n, pt: (0, 0))
    grid_spec = pltpu.PrefetchScalarGridSpec(
        num_scalar_prefetch=1, grid=(dbsz, n_blk),
        in_specs=[tok_spec(rm, HEAD_DIM), tok_spec(rd, hw), tok_spec(rm, HEAD_DIM), tok_spec(rm, HEAD_DIM),
                  tok_spec(rd, hw), tok_spec(rd, hw)] + mp + mp + dp + dp
        + [pl.BlockSpec((1, hw), lambda s, n, pt: (0, 0)), vec, vec, vec, vec],
        out_specs=[tok_spec(rm, HEAD_DIM), tok_spec(rd, hw)],
        scratch_shapes=[pltpu.VMEM((rm, LANES), F32), pltpu.VMEM((rm, LANES), F32),
                        pltpu.VMEM((rm, LANES), F32), pltpu.VMEM((n_blk, rm, HEAD_DIM), F32),
                        pltpu.VMEM((2, rd, 1), F32), pltpu.VMEM((2, rd, 1), F32),
                        pltpu.VMEM((2, rd, hw), F32)])
    return pl.pallas_call(
        functools.partial(_sample_attn_body, t_len=t_len), grid_spec=grid_spec,
        out_shape=[jax.ShapeDtypeStruct((dbsz, rm, HEAD_DIM), F32), jax.ShapeDtypeStruct((dbsz, rd, hw), F32)],
        compiler_params=_cparams("parallel", "arbitrary"), name="sample_attn")(
            page_table.reshape(-1), qm, qd, kmn, vmn, kdn, vdn,
            cmk, cmk, cmv, cmv, cdk, cdk, cdv, cdv, g, *lams)


def _ln1_router_body(x_ref, a_ref, gi_ref, bi_ref, g1_ref, b1_ref, wh_ref, wl_ref, br_ref,
                     h1_ref, gates_ref, sel_ref):
    h = _layer_norm(x_ref[...], gi_ref[...], bi_ref[...])
    z = _layer_norm(ALPHA * h + a_ref[...], g1_ref[...], b1_ref[...])
    h1_ref[...] = z
    z_hi = z.astype(BF16)
    z_lo = (z - z_hi.astype(F32)).astype(BF16)
    w_hi = wh_ref[...]
    logits = _dot_nt(w_hi, z_hi) + _dot_nt(w_hi, z_lo) + _dot_nt(wl_ref[...], z_hi)
    scores = _sigmoid(logits)
    choice = scores + br_ref[...]
    n_exp, tm = choice.shape
    per = n_exp // N_GROUPS
    c3 = choice.reshape(N_GROUPS, per, tm)
    sub = lax.broadcasted_iota(jnp.int32, c3.shape, 1).astype(F32)
    top2 = _pick_top(c3, sub, 2, 1, float(per))
    grp = jnp.sum(jnp.where(top2, c3, 0.0), axis=1)
    gidx = lax.broadcasted_iota(jnp.int32, grp.shape, 0).astype(F32)
    keep_g = _pick_top(grp, gidx, TOPK_GROUPS, 0, float(N_GROUPS))
    keep = jnp.broadcast_to(keep_g[:, None, :], c3.shape).reshape(n_exp, tm)
    eidx = lax.broadcasted_iota(jnp.int32, choice.shape, 0).astype(F32)
    sel = _pick_top(jnp.where(keep, choice, NEG), eidx, TOP_K, 0, float(n_exp))
    wsel = jnp.where(sel, scores, 0.0)
    gates_ref[...] = wsel / jnp.sum(wsel, axis=0, keepdims=True) * ROUTED_SCALE
    sel_ref[...] = jnp.where(sel, 1.0, 0.0)


def _ln1_router(x, a, gi, bi, g1, b1, w_router, b_router):
    n, d = x.shape
    n_exp = w_router.shape[1]
    wt = w_router.T
    w_hi = wt.astype(BF16)
    w_lo = (wt - w_hi.astype(F32)).astype(BF16)
    row = pl.BlockSpec((ROW_TILE, d), lambda i: (i, 0))
    vec = pl.BlockSpec((1, d), lambda i: (0, 0))
    wspec = pl.BlockSpec((n_exp, d), lambda i: (0, 0))
    ex = pl.BlockSpec((n_exp, ROW_TILE), lambda i: (0, i))
    return pl.pallas_call(
        _ln1_router_body, grid=(n // ROW_TILE,),
        in_specs=[row, row, vec, vec, vec, vec, wspec, wspec, pl.BlockSpec((n_exp, 1), lambda i: (0, 0))],
        out_specs=[row, ex, ex],
        out_shape=[jax.ShapeDtypeStruct((n, d), F32), jax.ShapeDtypeStruct((n_exp, n), F32),
                   jax.ShapeDtypeStruct((n_exp, n), F32)],
        compiler_params=_cparams("parallel"), name="ln1_router")(
            x, a, gi.reshape(1, d), bi.reshape(1, d), g1.reshape(1, d), b1.reshape(1, d),
            w_hi, w_lo, b_router.reshape(n_exp, 1))


def _dispatch_tables(sel_t, gates_t):
    n_exp, n_tok = sel_t.shape
    n_asg = n_tok * TOP_K
    nb_max = n_asg // MOE_TM + n_exp
    mask = sel_t > 0.5
    mi = mask.astype(jnp.int32)
    rank = jnp.cumsum(mi, axis=1) - mi
    counts = jnp.sum(mi, axis=1)
    nb = (counts + MOE_TM - 1) // MOE_TM
    nb = nb.at[n_exp - 1].add(nb_max - jnp.sum(nb))
    b_end = jnp.cumsum(nb)
    b_start = b_end - nb
    slot = b_start[:, None] * MOE_TM + rank
    _, tok_exp = lax.top_k(sel_t.T, TOP_K)
    tok_slots = jnp.take_along_axis(slot.T, tok_exp, axis=1)
    tok_gates = jnp.take_along_axis(gates_t.T, tok_exp, axis=1)
    order = jnp.argsort(tok_exp.reshape(-1), stable=True)
    sorted_tok = (order // TOP_K).astype(jnp.int32)
    sorted_gate = tok_gates.reshape(-1)[order]
    seg_start = jnp.cumsum(counts) - counts
    blocks = jnp.arange(nb_max, dtype=jnp.int32)
    block_expert = jnp.minimum(jnp.searchsorted(b_end, blocks, side="right"), n_exp - 1).astype(jnp.int32)
    slots = jnp.arange(nb_max * MOE_TM, dtype=jnp.int32)
    s_exp = block_expert[slots // MOE_TM]
    s_rank = slots - b_start[s_exp] * MOE_TM
    live = s_rank < counts[s_exp]
    src = jnp.clip(seg_start[s_exp] + s_rank, 0, n_asg - 1)
    slot_tok = jnp.where(live, sorted_tok[src], 0).astype(jnp.int32)
    slot_gate = jnp.where(live, sorted_gate[src], 0.0).astype(F32)
    return slot_tok, slot_gate, tok_slots.astype(jnp.int32), block_expert, b_start.astype(jnp.int32), nb


def _step_tables(block_expert, b_start, nb, n_chunks):
    nb_max = block_expert.shape[0]
    steps = jnp.arange(nb_max * n_chunks, dtype=jnp.int32)
    s_end = jnp.cumsum(nb) * n_chunks
    e = jnp.minimum(jnp.searchsorted(s_end, steps, side="right"), nb.shape[0] - 1).astype(jnp.int32)
    local = steps - b_start[e] * n_chunks
    nbe = jnp.maximum(nb[e], 1)
    chunk = (local // nbe).astype(jnp.int32)
    within = local % nbe
    rb = (b_start[e] + within).astype(jnp.int32)
    first = (within == 0).astype(jnp.int32)
    return rb, chunk, e, first


def _row_copy(src_hbm, dst_hbm, src_row, dst_row, sem):
    return pltpu.make_async_copy(src_hbm.at[pl.ds(src_row, 1)], dst_hbm.at[pl.ds(dst_row, 1)], sem)


def _gather_rows_body(idx_ref, src_hbm, dst_hbm, sem):
    i = pl.program_id(0)
    n_rows = idx_ref.shape[2]
    base = i * n_rows

    def issue(r, carry):
        _row_copy(src_hbm, dst_hbm, idx_ref[0, 0, r], base + r, sem.at[0]).start()
        return carry

    lax.fori_loop(0, n_rows, issue, 0, unroll=8)

    def drain(r, carry):
        _row_copy(src_hbm, dst_hbm, 0, base + r, sem.at[0]).wait()
        return carry

    lax.fori_loop(0, n_rows, drain, 0, unroll=8)


def _gather_rows(src, slot_tok):
    d = src.shape[1]
    n_slots = slot_tok.shape[0]
    nb = n_slots // MOE_TM
    return pl.pallas_call(
        _gather_rows_body, grid=(nb,),
        in_specs=[pl.BlockSpec((1, 1, MOE_TM), lambda i: (i, 0, 0), memory_space=pltpu.SMEM),
                  pl.BlockSpec(memory_space=pl.ANY)],
        out_specs=pl.BlockSpec(memory_space=pl.ANY),
        out_shape=jax.ShapeDtypeStruct((n_slots, d), src.dtype),
        scratch_shapes=[pltpu.SemaphoreType.DMA((1,))],
        compiler_params=_cparams("arbitrary"), name="moe_gather")(slot_tok.reshape(nb, 1, MOE_TM), src)


def _moe_up_body(rb_ref, ch_ref, e_ref, first_ref, x_ref, wg_ref, wu_ref, hb_ref, wgb_ref, wub_ref):
    s = pl.program_id(0)

    @pl.when(first_ref[s] == 1)
    def _cast():
        wgb_ref[...] = wg_ref[0].astype(BF16)
        wub_ref[...] = wu_ref[0].astype(BF16)

    x = x_ref[...].astype(BF16)
    g = _dot(x, wgb_ref[...])
    u = _dot(x, wub_ref[...])
    hb_ref[...] = (g * _sigmoid(g) * u).astype(hb_ref.dtype)


def _moe_up(x, wg, wu, tables):
    rb, chunk, e, first = tables
    d = x.shape[1]
    ff = wg.shape[2]
    wspec = pl.BlockSpec((1, d, MOE_TF), lambda s, rb, ch, e, f: (e[s], 0, ch[s]))
    grid_spec = pltpu.PrefetchScalarGridSpec(
        num_scalar_prefetch=4, grid=(rb.shape[0],),
        in_specs=[pl.BlockSpec((MOE_TM, d), lambda s, rb, ch, e, f: (rb[s], 0)), wspec, wspec],
        out_specs=pl.BlockSpec((MOE_TM, MOE_TF), lambda s, rb, ch, e, f: (rb[s], ch[s])),
        scratch_shapes=[pltpu.VMEM((d, MOE_TF), BF16), pltpu.VMEM((d, MOE_TF), BF16)])
    return pl.pallas_call(
        _moe_up_body, grid_spec=grid_spec, out_shape=jax.ShapeDtypeStruct((x.shape[0], ff), BF16),
        compiler_params=_cparams("arbitrary"), name="moe_up")(rb, chunk, e, first, x, wg, wu)


def _moe_down_body(rb_ref, e_ref, first_ref, hb_ref, wd_ref, gate_ref, y_ref, wdb_ref):
    s = pl.program_id(0)

    @pl.when(first_ref[s] == 1)
    def _cast():
        wdb_ref[...] = wd_ref[0].astype(BF16)

    y_ref[...] = _dot(hb_ref[...], wdb_ref[...]) * gate_ref[...]


def _moe_down(hb, wd, gate, tables):
    rb, _, e, first = tables
    ff = hb.shape[1]
    d = wd.shape[2]
    grid_spec = pltpu.PrefetchScalarGridSpec(
        num_scalar_prefetch=3, grid=(rb.shape[0],),
        in_specs=[pl.BlockSpec((MOE_TM, ff), lambda s, rb, e, f: (rb[s], 0)),
                  pl.BlockSpec((1, ff, d), lambda s, rb, e, f: (e[s], 0, 0)),
                  pl.BlockSpec((MOE_TM, 1), lambda s, rb, e, f: (rb[s], 0))],
        out_specs=pl.BlockSpec((MOE_TM, d), lambda s, rb, e, f: (rb[s], 0)),
        scratch_shapes=[pltpu.VMEM((ff, d), BF16)])
    return pl.pallas_call(
        _moe_down_body, grid_spec=grid_spec, out_shape=jax.ShapeDtypeStruct((hb.shape[0], d), F32),
        compiler_params=_cparams("arbitrary"), name="moe_down")(rb, e, first, hb, wd, gate)


def _swiglu_grouped(x, wg, wu, wd, gate, block_expert, b_start, nb):
    up_tables = _step_tables(block_expert, b_start, nb, wg.shape[2] // MOE_TF)
    hb = _moe_up(x, wg, wu, up_tables)
    return _moe_down(hb, wd, gate, _step_tables(block_expert, b_start, nb, 1))


def _combine_body(cur_ref, nxt_ref, y_hbm, h1_ref, ysh_ref, g_ref, b_ref, o_ref, gbuf, sem):
    i = pl.program_id(0)
    n = pl.num_programs(0)
    n_rows = cur_ref.shape[2]

    def fetch(idx_ref, buf, r):
        return pltpu.make_async_copy(y_hbm.at[pl.ds(idx_ref[0, 0, r], 1)],
                                     gbuf.at[buf, r % TOP_K, pl.ds(r // TOP_K, 1)], sem.at[buf])

    def issue(idx_ref, buf):
        def body(r, carry):
            fetch(idx_ref, buf, r).start()
            return carry
        lax.fori_loop(0, n_rows, body, 0, unroll=8)

    @pl.when(i == 0)
    def _first():
        issue(cur_ref, 0)

    @pl.when(i + 1 < n)
    def _ahead():
        issue(nxt_ref, (i + 1) % 2)

    buf = i % 2

    def drain(r, carry):
        fetch(cur_ref, buf, r).wait()
        return carry

    lax.fori_loop(0, n_rows, drain, 0, unroll=8)
    routed = gbuf[buf, 0]
    for k in range(1, TOP_K):
        routed = routed + gbuf[buf, k]
    o_ref[...] = _layer_norm(ALPHA * h1_ref[...] + (routed + ysh_ref[...]), g_ref[...], b_ref[...])


def _combine_ln2(y, tok_slots, h1, ysh, g, b):
    n, d = h1.shape
    tt = COMBINE_TT
    nt = n // tt
    idx = tok_slots.reshape(nt, 1, tt * TOP_K)
    row = pl.BlockSpec((tt, d), lambda i: (i, 0))
    vec = pl.BlockSpec((1, d), lambda i: (0, 0))
    return pl.pallas_call(
        _combine_body, grid=(nt,),
        in_specs=[pl.BlockSpec((1, 1, tt * TOP_K), lambda i: (i, 0, 0), memory_space=pltpu.SMEM),
                  pl.BlockSpec((1, 1, tt * TOP_K), lambda i: (jnp.minimum(i + 1, nt - 1), 0, 0),
                               memory_space=pltpu.SMEM),
                  pl.BlockSpec(memory_space=pl.ANY), row, row, vec, vec],
        out_specs=row, out_shape=jax.ShapeDtypeStruct((n, d), F32),
        scratch_shapes=[pltpu.VMEM((2, TOP_K, tt, d), F32), pltpu.SemaphoreType.DMA((2,))],
        compiler_params=_cparams("arbitrary"), name="moe_combine_ln2")(
            idx, idx, y, h1, ysh, g.reshape(1, d), b.reshape(1, d))


def _forward(x_prompt, x_sample, cache_moba_k, cache_moba_v, cache_diff_k, cache_diff_v, page_table,
             ln_in_g, ln_in_b, w_in, diff_lq1, diff_lk1, diff_lq2, diff_lk2, diff_norm_g, w_out,
             ln1_g, ln1_b, w_router, b_router, w_exp_gate, w_exp_up, w_exp_down,
             w_sh_gate, w_sh_up, w_sh_down, ln2_g, ln2_b):
    bsz, s_len, d = x_prompt.shape
    dbsz, t_len, _ = x_sample.shape
    n_p = bsz * s_len
    n_s = dbsz * t_len
    n = n_p + n_s
    mw = d // 2
    m_heads = mw // HEAD_DIM
    d_heads = mw // (2 * HEAD_DIM)
    past = page_table.shape[1] * cache_moba_k.shape[2]
    lyr = 0

    x = jnp.concatenate([x_prompt.reshape(n_p, d), x_sample.reshape(n_s, d)], axis=0)
    pos = jnp.concatenate([jnp.tile(jnp.arange(s_len, dtype=jnp.int32), bsz),
                           jnp.tile(past + jnp.arange(t_len, dtype=jnp.int32), dbsz)])
    tabs = _rope_tables(pos)
    h = _ln_in(x, ln_in_g, ln_in_b)
    w_in_b = w_in[lyr].astype(BF16)
    qscale = 1.0 / math.sqrt(HEAD_DIM)
    (qm,) = _proj(h, w_in_b, 0, mw, (BF16,), tabs, qscale, "proj_qm")
    km, km_b = _proj(h, w_in_b, mw, mw, (F32, BF16), tabs, 1.0, "proj_km")
    vm, vm_b = _proj(h, w_in_b, 2 * mw, mw, (F32, BF16), None, 1.0, "proj_vm")
    (qd,) = _proj(h, w_in_b, 3 * mw, mw, (BF16,), tabs, qscale, "proj_qd")
    kd, kd_b = _proj(h, w_in_b, 4 * mw, mw, (F32, BF16), tabs, 1.0, "proj_kd")
    vd, vd_b = _proj(h, w_in_b, 5 * mw, mw, (F32, BF16), None, 1.0, "proj_vd")

    lams = [v[lyr].reshape(1, HEAD_DIM) for v in (diff_lq1, diff_lk1, diff_lq2, diff_lk2)]
    g_diff = diff_norm_g[lyr].reshape(1, 2 * HEAD_DIM)

    nbp = s_len // MOBA_BLOCK
    means = _moba_means(km, n_p).reshape(bsz, nbp, mw)
    means = jnp.pad(means, ((0, 0), (0, LANES - nbp), (0, 0)))
    om_p = _moba_prompt(qm, km_b, vm_b, means, bsz, s_len)
    od_p = _diff_prompt(qd, kd_b, vd_b, g_diff, lams, bsz, s_len)

    def smp(a, w):
        return a[n_p:].reshape(dbsz, t_len * (mw // w), w)

    hw = 2 * HEAD_DIM
    om_s, od_s = _sample_attn(page_table, smp(qm, HEAD_DIM), smp(qd, hw), smp(km, HEAD_DIM), smp(vm, HEAD_DIM),
                              smp(kd, hw), smp(vd, hw), cache_moba_k, cache_moba_v, cache_diff_k,
                              cache_diff_v, g_diff, lams, lyr, t_len)
    o_s = jnp.concatenate([om_s.reshape(n_s, mw), od_s.reshape(n_s, mw)], axis=1)

    o_all = jnp.concatenate([jnp.concatenate([om_p, od_p], axis=1), o_s.astype(BF16)], axis=0)
    (a,) = _proj(o_all, w_out[lyr].astype(BF16), 0, d, (F32,), None, 1.0, "proj_out")
    h1, gates_t, sel_t = _ln1_router(x, a, ln_in_g, ln_in_b, ln1_g[lyr], ln1_b[lyr],
                                     w_router[lyr], b_router[lyr])

    slot_tok, slot_gate, tok_slots, block_expert, b_start, nb = _dispatch_tables(sel_t, gates_t)
    xs = _gather_rows(h1, slot_tok)
    y = _swiglu_grouped(xs, w_exp_gate[lyr], w_exp_up[lyr], w_exp_down[lyr], slot_gate.reshape(-1, 1),
                        block_expert, b_start, nb)
    nb_sh = n // MOE_TM
    ysh = _swiglu_grouped(h1, w_sh_gate[lyr][None], w_sh_up[lyr][None], w_sh_down[lyr][None],
                          jnp.ones((n, 1), F32), jnp.zeros((nb_sh,), jnp.int32),
                          jnp.zeros((1,), jnp.int32), jnp.full((1,), nb_sh, jnp.int32))
    out = _combine_ln2(y, tok_slots, h1, ysh, ln2_g[lyr], ln2_b[lyr])

    def prm(t, nh):
        return t[:n_p].reshape(1, bsz, s_len, nh, mw // nh)

    def dec(t, nh):
        return t[n_p:].reshape(1, dbsz, t_len, nh, mw // nh)

    outs = (out[:n_p].reshape(bsz, s_len, d), out[n_p:].reshape(dbsz, t_len, d),
            prm(km, m_heads), prm(vm, m_heads), prm(kd, d_heads), prm(vd, d_heads),
            dec(km, m_heads), dec(vm, m_heads), dec(kd, d_heads), dec(vd, d_heads))
    mid = dict(a=a, h1=h1, y=y, tok_slots=tok_slots, ysh=ysh, gates_t=gates_t, sel_t=sel_t)
    return outs, mid


def kernel(x_prompt, x_sample, cache_moba_k, cache_moba_v, cache_diff_k, cache_diff_v, page_table, ln_in_g, ln_in_b, w_in, diff_lq1, diff_lk1, diff_lq2, diff_lk2, diff_norm_g, w_out, ln1_g, ln1_b, w_router, b_router, w_exp_gate, w_exp_up, w_exp_down, w_sh_gate, w_sh_up, w_sh_down, ln2_g, ln2_b):
    outs, _ = _forward(x_prompt, x_sample, cache_moba_k, cache_moba_v, cache_diff_k, cache_diff_v, page_table,
                       ln_in_g, ln_in_b, w_in, diff_lq1, diff_lk1, diff_lq2, diff_lk2, diff_norm_g, w_out,
                       ln1_g, ln1_b, w_router, b_router, w_exp_gate, w_exp_up, w_exp_down,
                       w_sh_gate, w_sh_up, w_sh_down, ln2_g, ln2_b)
    return outs
```

```python
import functools
import math

import jax
import jax.numpy as jnp
from jax import lax
from jax.experimental import pallas as pl
from jax.experimental.pallas import tpu as pltpu

F32 = jnp.float32
BF16 = jnp.bfloat16

HEAD_DIM = 128
MOBA_BLOCK = 256
MOBA_TOPK = 3
ROPE_THETA = 500000.0
ROPE_DIV = 4
N_GROUPS = 8
TOPK_GROUPS = 4
TOP_K = 8
ROUTED_SCALE = 2.5
LN_EPS = 1e-5
NEG = -1e30
DEPTH = 1
ALPHA = (2.0 * DEPTH) ** 0.25
LAM_INIT = 0.8 - 0.6 * math.exp(-0.3 * 0)

VMEM_LIMIT_BYTES = 56 * 1024 * 1024
LANES = 128

ROW_TILE = 256
MM_TM = 512
MM_TN = 512
ATT_TQ = MOBA_BLOCK
MOBA_HEADS_PER_STEP = 4
DIFF_HEADS_PER_STEP = 2
MOE_TM = 256
MOE_TF = 512
SCATTER_TT = 64
COMBINE_TT = 32


def _cparams(*sem):
    return pltpu.CompilerParams(dimension_semantics=sem, vmem_limit_bytes=VMEM_LIMIT_BYTES)


def _dot(a, b):
    return jnp.dot(a, b, preferred_element_type=F32)


def _dot_nt(a, b):
    return lax.dot_general(a, b, (((1,), (1,)), ((), ())), preferred_element_type=F32)


def _layer_norm(x, g, b):
    mu = jnp.mean(x, axis=-1, keepdims=True)
    xc = x - mu
    var = jnp.mean(xc * xc, axis=-1, keepdims=True)
    return xc * lax.rsqrt(var + LN_EPS) * g + b


def _sigmoid(x):
    return 1.0 / (1.0 + jnp.exp(-x))


def _ln_in_body(x_ref, g_ref, b_ref, o_ref):
    o_ref[...] = _layer_norm(x_ref[...], g_ref[...], b_ref[...]).astype(o_ref.dtype)


def _ln_in(x, g, b):
    n, d = x.shape
    row = pl.BlockSpec((ROW_TILE, d), lambda i: (i, 0))
    vec = pl.BlockSpec((1, d), lambda i: (0, 0))
    return pl.pallas_call(
        _ln_in_body, grid=(n // ROW_TILE,), in_specs=[row, vec, vec], out_specs=row,
        out_shape=jax.ShapeDtypeStruct((n, d), BF16), compiler_params=_cparams("parallel"),
        name="ln_in")(x, g.reshape(1, d), b.reshape(1, d))


def _rope_chunk(y, c, sa, sb):
    return y * c + pltpu.roll(y, HEAD_DIM - 16, 1) * sa + pltpu.roll(y, 16, 1) * sb


def _proj_body(*refs, rope, scale, n_out):
    if rope:
        h_ref, w_ref, c_ref, sa_ref, sb_ref = refs[:5]
    else:
        h_ref, w_ref = refs[:2]
    out_refs = refs[len(refs) - n_out:]
    acc = _dot(h_ref[...], w_ref[...])
    if rope:
        c, sa, sb = c_ref[...], sa_ref[...], sb_ref[...]
        acc = jnp.concatenate(
            [_rope_chunk(acc[:, k * HEAD_DIM:(k + 1) * HEAD_DIM], c, sa, sb)
             for k in range(acc.shape[1] // HEAD_DIM)], axis=1)
    if scale != 1.0:
        acc = acc * scale
    for o_ref in out_refs:
        o_ref[...] = acc.astype(o_ref.dtype)


def _proj(h, w, col0, ncols, out_dtypes, rope_tabs=None, scale=1.0, name="proj"):
    n, k = h.shape
    tm, tn = MM_TM, MM_TN
    cb = col0 // tn
    in_specs = [pl.BlockSpec((tm, k), lambda i, j: (i, 0)),
                pl.BlockSpec((k, tn), lambda i, j: (0, cb + j))]
    args = [h, w]
    if rope_tabs is not None:
        in_specs += [pl.BlockSpec((tm, HEAD_DIM), lambda i, j: (i, 0))] * 3
        args += list(rope_tabs)
    out_spec = pl.BlockSpec((tm, tn), lambda i, j: (i, j))
    outs = pl.pallas_call(
        functools.partial(_proj_body, rope=rope_tabs is not None, scale=scale, n_out=len(out_dtypes)),
        grid=(n // tm, ncols // tn), in_specs=in_specs, out_specs=[out_spec] * len(out_dtypes),
        out_shape=[jax.ShapeDtypeStruct((n, ncols), dt) for dt in out_dtypes],
        compiler_params=_cparams("parallel", "arbitrary"), name=name)(*args)
    return outs


def _rope_tables(pos):
    rot = HEAD_DIM // ROPE_DIV
    half = rot // 2
    inv_freq = ROPE_THETA ** (-jnp.arange(half, dtype=F32) * (2.0 / rot))
    ang = pos.astype(F32)[:, None] * inv_freq[None, :]
    cos, sin = jnp.cos(ang), jnp.sin(ang)
    n = pos.shape[0]
    ones = jnp.ones((n, HEAD_DIM - rot), F32)
    zeros = jnp.zeros((n, HEAD_DIM - rot), F32)
    zh = jnp.zeros((n, half), F32)
    c = jnp.concatenate([cos, cos, ones], axis=1)
    sa = jnp.concatenate([-sin, zh, zeros], axis=1)
    sb = jnp.concatenate([zh, sin, zeros], axis=1)
    return c, sa, sb


def _means_body(k_ref, o_ref):
    o_ref[0] = jnp.mean(k_ref[...], axis=0, keepdims=True)


def _moba_means(k, n_rows):
    w = k.shape[1]
    nb = n_rows // MOBA_BLOCK
    return pl.pallas_call(
        _means_body, grid=(nb,), in_specs=[pl.BlockSpec((MOBA_BLOCK, w), lambda i: (i, 0))],
        out_specs=pl.BlockSpec((1, 1, w), lambda i: (i, 0, 0)),
        out_shape=jax.ShapeDtypeStruct((nb, 1, w), F32), compiler_params=_cparams("parallel"),
        name="moba_means")(k)


def _pick_top(cur, idx_f, n_pick, axis, big):
    sel = jnp.zeros(cur.shape, jnp.bool_)
    for _ in range(n_pick):
        mx = jnp.max(cur, axis=axis, keepdims=True)
        first = jnp.min(jnp.where(cur == mx, idx_f, big), axis=axis, keepdims=True)
        pick = idx_f == first
        sel = jnp.logical_or(sel, pick)
        cur = jnp.where(pick, -jnp.inf, cur)
    return sel


def _row_reduce(x, op, reduce):
    if x.shape[1] % LANES:
        return reduce(x, axis=1, keepdims=True)
    folded = x[:, :LANES]
    for j in range(1, x.shape[1] // LANES):
        folded = op(folded, x[:, j * LANES:(j + 1) * LANES])
    return reduce(folded, axis=1, keepdims=True)


def _softmax_step(s, v, m_ref, l_ref, acc_ref, c, first):
    if first:
        m_new = _row_reduce(s, jnp.maximum, jnp.max)
        p = jnp.exp(s - m_new)
        l_ref[c] = _row_reduce(p, jnp.add, jnp.sum)
        acc_ref[c] = _dot(p.astype(BF16), v)
    else:
        m_old = m_ref[c]
        m_new = jnp.maximum(m_old, _row_reduce(s, jnp.maximum, jnp.max))
        alpha = jnp.exp(m_old - m_new)
        p = jnp.exp(s - m_new)
        l_ref[c] = alpha * l_ref[c] + _row_reduce(p, jnp.add, jnp.sum)
        acc_ref[c] = alpha * acc_ref[c] + _dot(p.astype(BF16), v)
    m_ref[c] = m_new


def _moba_prompt_body(q_ref, k_ref, v_ref, mean_ref, o_ref, sel_ref, m_ref, l_ref, acc_ref):
    qi = pl.program_id(2)
    tq = q_ref.shape[0]
    n_h = q_ref.shape[1] // HEAD_DIM
    lane = lax.broadcasted_iota(jnp.int32, (tq, LANES), 1)
    past = lane < qi
    row = lax.broadcasted_iota(jnp.int32, (tq, tq), 0)
    col = lax.broadcasted_iota(jnp.int32, (tq, tq), 1)
    own = pl.ds(pl.multiple_of(qi * tq, tq), tq)
    hs = [slice(h * HEAD_DIM, (h + 1) * HEAD_DIM) for h in range(n_h)]
    for h in range(n_h):
        q = q_ref[:, hs[h]]
        mean = mean_ref[0, :, hs[h]]
        m_hi = mean.astype(BF16)
        m_lo = (mean - m_hi.astype(F32)).astype(BF16)
        gate = _dot_nt(q, m_hi) + _dot_nt(q, m_lo)
        sel = _pick_top(jnp.where(past, gate, -jnp.inf), lane.astype(F32), MOBA_TOPK, 1, float(LANES))
        sel_ref[h] = jnp.where(jnp.logical_and(sel, past), 1.0, 0.0).astype(BF16)
        s = jnp.where(col <= row, _dot_nt(q, k_ref[own, hs[h]]), NEG)
        _softmax_step(s, v_ref[own, hs[h]], m_ref, l_ref, acc_ref, h, True)

    blk_id = lax.broadcasted_iota(jnp.int32, (LANES, tq), 0)

    def body(ki, carry):
        blk = pl.ds(pl.multiple_of(ki * tq, tq), tq)
        spread = jnp.where(blk_id == ki, 1.0, 0.0).astype(BF16)
        for h in range(n_h):
            picked = _dot(sel_ref[h], spread) > 0.5
            s = jnp.where(picked, _dot_nt(q_ref[:, hs[h]], k_ref[blk, hs[h]]), NEG)
            _softmax_step(s, v_ref[blk, hs[h]], m_ref, l_ref, acc_ref, h, False)
        return carry

    lax.fori_loop(0, qi, body, 0)
    for h in range(n_h):
        o_ref[:, hs[h]] = (acc_ref[h] / l_ref[h]).astype(o_ref.dtype)


def _moba_prompt(q, k, v, means, bsz, s_len):
    hb = MOBA_HEADS_PER_STEP
    w = hb * HEAD_DIM
    groups = q.shape[1] // w
    nq = s_len // ATT_TQ
    kv_spec = pl.BlockSpec((s_len, w), lambda b, h, i: (b, h))
    return pl.pallas_call(
        _moba_prompt_body, grid=(bsz, groups, nq),
        in_specs=[pl.BlockSpec((ATT_TQ, w), lambda b, h, i: (b * nq + i, h)), kv_spec, kv_spec,
                  pl.BlockSpec((1, LANES, w), lambda b, h, i: (b, 0, h))],
        out_specs=pl.BlockSpec((ATT_TQ, w), lambda b, h, i: (b * nq + i, h)),
        out_shape=jax.ShapeDtypeStruct((bsz * s_len, q.shape[1]), BF16),
        scratch_shapes=[pltpu.VMEM((hb, ATT_TQ, LANES), BF16), pltpu.VMEM((hb, ATT_TQ, 1), F32),
                        pltpu.VMEM((hb, ATT_TQ, 1), F32), pltpu.VMEM((hb, ATT_TQ, HEAD_DIM), F32)],
        compiler_params=_cparams("parallel", "parallel", "arbitrary"), name="moba_prompt")(q, k, v, means)


def _lambda(lq1, lk1, lq2, lk2):
    return (jnp.exp(jnp.sum(lq1 * lk1, axis=1, keepdims=True))
            - jnp.exp(jnp.sum(lq2 * lk2, axis=1, keepdims=True)) + LAM_INIT)


def _diff_out(acc0, l0, acc1, l1, lam, g):
    o = acc0 / l0 - lam * (acc1 / l1)
    ms = jnp.mean(o * o, axis=1, keepdims=True)
    return o * lax.rsqrt(ms + LN_EPS) * g * (1.0 - LAM_INIT)


def _diff_prompt_body(q_ref, k_ref, v_ref, g_ref, lq1, lk1, lq2, lk2, o_ref, m_ref, l_ref, acc_ref):
    qi = pl.program_id(2)
    tq = q_ref.shape[0]
    hw = 2 * HEAD_DIM
    n_h = q_ref.shape[1] // hw
    row = lax.broadcasted_iota(jnp.int32, (tq, tq), 0)
    col = lax.broadcasted_iota(jnp.int32, (tq, tq), 1)
    own = pl.ds(pl.multiple_of(qi * tq, tq), tq)

    def comp(h, c):
        return slice(h * hw + c * HEAD_DIM, h * hw + (c + 1) * HEAD_DIM)

    def head(h):
        return slice(h * hw, (h + 1) * hw)

    for h in range(n_h):
        for c in range(2):
            s = jnp.where(col <= row, _dot_nt(q_ref[:, comp(h, c)], k_ref[own, comp(h, c)]), NEG)
            _softmax_step(s, v_ref[own, head(h)], m_ref, l_ref, acc_ref, 2 * h + c, True)

    def body(ki, carry):
        blk = pl.ds(pl.multiple_of(ki * tq, tq), tq)
        for h in range(n_h):
            for c in range(2):
                s = _dot_nt(q_ref[:, comp(h, c)], k_ref[blk, comp(h, c)])
                _softmax_step(s, v_ref[blk, head(h)], m_ref, l_ref, acc_ref, 2 * h + c, False)
        return carry

    lax.fori_loop(0, qi, body, 0)
    lam = _lambda(lq1[...], lk1[...], lq2[...], lk2[...])
    for h in range(n_h):
        o_ref[:, head(h)] = _diff_out(acc_ref[2 * h], l_ref[2 * h], acc_ref[2 * h + 1], l_ref[2 * h + 1],
                                      lam, g_ref[...]).astype(o_ref.dtype)


def _diff_prompt(q, k, v, g, lams, bsz, s_len):
    hw = 2 * HEAD_DIM
    hb = DIFF_HEADS_PER_STEP
    w = hb * hw
    groups = q.shape[1] // w
    nq = s_len // ATT_TQ
    kv_spec = pl.BlockSpec((s_len, w), lambda b, h, i: (b, h))
    vec = pl.BlockSpec((1, HEAD_DIM), lambda b, h, i: (0, 0))
    return pl.pallas_call(
        _diff_prompt_body, grid=(bsz, groups, nq),
        in_specs=[pl.BlockSpec((ATT_TQ, w), lambda b, h, i: (b * nq + i, h)), kv_spec, kv_spec,
                  pl.BlockSpec((1, hw), lambda b, h, i: (0, 0)), vec, vec, vec, vec],
        out_specs=pl.BlockSpec((ATT_TQ, w), lambda b, h, i: (b * nq + i, h)),
        out_shape=jax.ShapeDtypeStruct((bsz * s_len, q.shape[1]), BF16),
        scratch_shapes=[pltpu.VMEM((2 * hb, ATT_TQ, 1), F32), pltpu.VMEM((2 * hb, ATT_TQ, 1), F32),
                        pltpu.VMEM((2 * hb, ATT_TQ, hw), F32)],
        compiler_params=_cparams("parallel", "parallel", "arbitrary"), name="diff_prompt")(q, k, v, g, *lams)


def _sample_attn_body(pt_ref, qm_ref, qd_ref, kmn_ref, vmn_ref, kdn_ref, vdn_ref,
                      mk0, mk1, mv0, mv1, dk0, dk1, dv0, dv1, g_ref, lq1, lk1, lq2, lk2,
                      om_ref, od_ref, gate_ref, mall_ref, lall_ref, oblk_ref, md_ref, ld_ref, accd_ref,
                      *, t_len):
    del pt_ref
    n = pl.program_id(1)
    n_blk = pl.num_programs(1)
    page, m_heads, _ = mk0.shape[2:]
    d_heads, hw = dk0.shape[3:]
    qm = qm_ref[0]
    qd = qd_ref[0]
    rm, rd = qm.shape[0], qd.shape[0]
    lane = lax.broadcasted_iota(jnp.int32, (rm, LANES), 1)

    def head_match(rows, cols, heads):
        r = lax.broadcasted_iota(jnp.int32, (rows, cols), 0)
        c = lax.broadcasted_iota(jnp.int32, (rows, cols), 1)
        return r, c, (r % heads) == (c % heads)

    def rows2d(ref):
        x = ref[0, 0]
        return x.reshape(x.shape[0] * x.shape[1], x.shape[2])

    @pl.when(n == 0)
    def _init():
        gate_ref[...] = jnp.zeros(gate_ref.shape, F32)
        mall_ref[...] = jnp.zeros(mall_ref.shape, F32)
        lall_ref[...] = jnp.zeros(lall_ref.shape, F32)
        md_ref[...] = jnp.full(md_ref.shape, NEG, F32)
        ld_ref[...] = jnp.zeros(ld_ref.shape, F32)
        accd_ref[...] = jnp.zeros(accd_ref.shape, F32)

    k0, k1 = mk0[0, 0], mk1[0, 0]
    mean = (jnp.sum(k0, axis=0) + jnp.sum(k1, axis=0)) * (1.0 / (2 * page))
    mean_rows = jnp.concatenate([mean] * t_len, axis=0)
    gate_n = jnp.sum(qm.astype(F32) * mean_rows, axis=1, keepdims=True)
    kb = jnp.concatenate([rows2d(mk0), rows2d(mk1)], axis=0).astype(BF16)
    vb = jnp.concatenate([rows2d(mv0), rows2d(mv1)], axis=0).astype(BF16)
    _, _, same = head_match(rm, kb.shape[0], m_heads)
    s = jnp.where(same, _dot_nt(qm, kb), NEG)
    m_n = _row_reduce(s, jnp.maximum, jnp.max)
    p = jnp.exp(s - m_n)
    l_n = _row_reduce(p, jnp.add, jnp.sum)
    oblk_ref[n] = _dot(p.astype(BF16), vb)
    here = lane == n
    gate_ref[...] = jnp.where(here, gate_n, gate_ref[...])
    mall_ref[...] = jnp.where(here, m_n, mall_ref[...])
    lall_ref[...] = jnp.where(here, l_n, lall_ref[...])

    kd = jnp.concatenate([rows2d(dk0), rows2d(dk1)], axis=0).astype(BF16)
    vd = jnp.concatenate([rows2d(dv0), rows2d(dv1)], axis=0).astype(BF16)
    _, _, same_d = head_match(rd, kd.shape[0], d_heads)

    def diff_update(c, s, v):
        m_old = md_ref[c]
        m_new = jnp.maximum(m_old, _row_reduce(s, jnp.maximum, jnp.max))
        alpha = jnp.exp(m_old - m_new)
        p = jnp.exp(s - m_new)
        ld_ref[c] = alpha * ld_ref[c] + _row_reduce(p, jnp.add, jnp.sum)
        accd_ref[c] = alpha * accd_ref[c] + _dot(p.astype(BF16), v)
        md_ref[c] = m_new

    for c in range(2):
        cs = slice(c * HEAD_DIM, (c + 1) * HEAD_DIM)
        diff_update(c, jnp.where(same_d, _dot_nt(qd[:, cs], kd[:, cs]), NEG), vd)

    @pl.when(n == n_blk - 1)
    def _finish():
        r, c, same = head_match(rm, rm, m_heads)
        own_ok = jnp.logical_and(same, c // m_heads <= r // m_heads)
        s_own = jnp.where(own_ok, _dot_nt(qm, kmn_ref[0].astype(BF16)), NEG)
        cached = lane < n_blk
        sel = _pick_top(jnp.where(cached, gate_ref[...], -jnp.inf), lane.astype(F32), MOBA_TOPK, 1,
                        float(LANES))
        sel = jnp.logical_and(sel, cached)
        mall = mall_ref[...]
        m_tot = jnp.maximum(jnp.max(jnp.where(sel, mall, NEG), axis=1, keepdims=True),
                            jnp.max(s_own, axis=1, keepdims=True))
        w_blk = jnp.where(sel, jnp.exp(mall - m_tot), 0.0)
        p_own = jnp.exp(s_own - m_tot)
        denom = (jnp.sum(w_blk * lall_ref[...], axis=1, keepdims=True)
                 + jnp.sum(p_own, axis=1, keepdims=True))
        o_m = _dot(p_own.astype(BF16), vmn_ref[0].astype(BF16))
        for b in range(oblk_ref.shape[0]):
            o_m = o_m + w_blk[:, b:b + 1] * oblk_ref[b]
        om_ref[0] = o_m / denom

        r, c, same = head_match(rd, rd, d_heads)
        own_ok = jnp.logical_and(same, c // d_heads <= r // d_heads)
        kdn = kdn_ref[0].astype(BF16)
        vdn = vdn_ref[0].astype(BF16)
        for comp in range(2):
            cs = slice(comp * HEAD_DIM, (comp + 1) * HEAD_DIM)
            diff_update(comp, jnp.where(own_ok, _dot_nt(qd[:, cs], kdn[:, cs]), NEG), vdn)
        lam = _lambda(lq1[...], lk1[...], lq2[...], lk2[...])
        od_ref[0] = _diff_out(accd_ref[0], ld_ref[0], accd_ref[1], ld_ref[1], lam, g_ref[...])


def _sample_attn(page_table, qm, qd, kmn, vmn, kdn, vdn, cmk, cmv, cdk, cdv, g, lams, lyr, t_len):
    dbsz, rm, _ = qm.shape
    rd, hw = qd.shape[1:]
    n_pages = page_table.shape[1]
    page, m_heads = cmk.shape[2:4]
    d_heads = cdk.shape[3]
    ppb = MOBA_BLOCK // page
    assert ppb == 2 and (n_pages * page) % MOBA_BLOCK == 0 and t_len <= MOBA_BLOCK
    n_blk = n_pages // ppb
    assert MOBA_TOPK <= n_blk <= LANES

    def tok_spec(rows, w):
        return pl.BlockSpec((1, rows, w), lambda s, n, pt: (s, 0, 0))

    def page_spec(j, heads, w):
        return pl.BlockSpec((1, 1, page, heads, w),
                            lambda s, n, pt: (lyr, pt[s * n_pages + ppb * n + j], 0, 0, 0))

    mp = [page_spec(0, m_heads, HEAD_DIM), page_spec(1, m_heads, HEAD_DIM)]
    dp = [page_spec(0, d_heads, hw), page_spec(1, d_heads, hw)]
    vec = pl.BlockSpec((1, HEAD_DIM), lambda s, n, pt: (0, 0))
    grid_spec = pltpu.PrefetchScalarGridSpec(
        num_scalar_prefetch=1, grid=(dbsz, n_blk),
        in_specs=[tok_spec(rm, HEAD_DIM), tok_spec(rd, hw), tok_spec(rm, HEAD_DIM), tok_spec(rm, HEAD_DIM),
                  tok_spec(rd, hw), tok_spec(rd, hw)] + mp + mp + dp + dp
        + [pl.BlockSpec((1, hw), lambda s, n, pt: (0, 0)), vec, vec, vec, vec],
        out_specs=[tok_spec(rm, HEAD_DIM), tok_spec(rd, hw)],
        scratch_shapes=[pltpu.VMEM((rm, LANES), F32), pltpu.VMEM((rm, LANES), F32),
                        pltpu.VMEM((rm, LANES), F32), pltpu.VMEM((n_blk, rm, HEAD_DIM), F32),
                        pltpu.VMEM((2, rd, 1), F32), pltpu.VMEM((2, rd, 1), F32),
                        pltpu.VMEM((2, rd, hw), F32)])
    return pl.pallas_call(
        functools.partial(_sample_attn_body, t_len=t_len), grid_spec=grid_spec,
        out_shape=[jax.ShapeDtypeStruct((dbsz, rm, HEAD_DIM), F32), jax.ShapeDtypeStruct((dbsz, rd, hw), F32)],
        compiler_params=_cparams("parallel", "arbitrary"), name="sample_attn")(
            page_table.reshape(-1), qm, qd, kmn, vmn, kdn, vdn,
            cmk, cmk, cmv, cmv, cdk, cdk, cdv, cdv, g, *lams)


def _ln1_router_body(x_ref, a_ref, gi_ref, bi_ref, g1_ref, b1_ref, wh_ref, wl_ref, br_ref,
                     h1_ref, exp_ref, gate_ref, sel_ref):
    h = _layer_norm(x_ref[...], gi_ref[...], bi_ref[...])
    z = _layer_norm(ALPHA * h + a_ref[...], g1_ref[...], b1_ref[...])
    h1_ref[...] = z
    z_hi = z.astype(BF16)
    z_lo = (z - z_hi.astype(F32)).astype(BF16)
    w_hi = wh_ref[...]
    logits = _dot_nt(w_hi, z_hi) + _dot_nt(w_hi, z_lo) + _dot_nt(wl_ref[...], z_hi)
    scores = _sigmoid(logits)
    choice = scores + br_ref[...]
    n_exp, tm = choice.shape
    per = n_exp // N_GROUPS
    c3 = choice.reshape(N_GROUPS, per, tm)
    sub = lax.broadcasted_iota(jnp.int32, c3.shape, 1).astype(F32)
    top2 = _pick_top(c3, sub, 2, 1, float(per))
    grp = jnp.sum(jnp.where(top2, c3, 0.0), axis=1)
    gidx = lax.broadcasted_iota(jnp.int32, grp.shape, 0).astype(F32)
    keep_g = _pick_top(grp, gidx, TOPK_GROUPS, 0, float(N_GROUPS))
    keep = jnp.broadcast_to(keep_g[:, None, :], c3.shape).reshape(n_exp, tm)
    eidx = lax.broadcasted_iota(jnp.int32, choice.shape, 0).astype(F32)
    cur = jnp.where(keep, choice, NEG)
    sel = jnp.zeros(cur.shape, jnp.bool_)
    ids, ws = [], []
    for _ in range(TOP_K):
        mx = jnp.max(cur, axis=0, keepdims=True)
        first = jnp.min(jnp.where(cur == mx, eidx, float(n_exp)), axis=0, keepdims=True)
        pick = eidx == first
        sel = jnp.logical_or(sel, pick)
        ids.append(first)
        ws.append(jnp.sum(jnp.where(pick, scores, 0.0), axis=0, keepdims=True))
        cur = jnp.where(pick, -jnp.inf, cur)
    wsel = jnp.concatenate(ws, axis=0)
    exp_ref[...] = jnp.concatenate(ids, axis=0).astype(jnp.int32)
    gate_ref[...] = wsel / jnp.sum(wsel, axis=0, keepdims=True) * ROUTED_SCALE
    sel_ref[...] = jnp.where(sel, 1.0, 0.0)


def _ln1_router(x, a, gi, bi, g1, b1, w_router, b_router):
    n, d = x.shape
    n_exp = w_router.shape[1]
    wt = w_router.T
    w_hi = wt.astype(BF16)
    w_lo = (wt - w_hi.astype(F32)).astype(BF16)
    row = pl.BlockSpec((ROW_TILE, d), lambda i: (i, 0))
    vec = pl.BlockSpec((1, d), lambda i: (0, 0))
    wspec = pl.BlockSpec((n_exp, d), lambda i: (0, 0))
    ex = pl.BlockSpec((n_exp, ROW_TILE), lambda i: (0, i))
    tk = pl.BlockSpec((TOP_K, ROW_TILE), lambda i: (0, i))
    return pl.pallas_call(
        _ln1_router_body, grid=(n // ROW_TILE,),
        in_specs=[row, row, vec, vec, vec, vec, wspec, wspec, pl.BlockSpec((n_exp, 1), lambda i: (0, 0))],
        out_specs=[row, tk, tk, ex],
        out_shape=[jax.ShapeDtypeStruct((n, d), F32), jax.ShapeDtypeStruct((TOP_K, n), jnp.int32),
                   jax.ShapeDtypeStruct((TOP_K, n), F32), jax.ShapeDtypeStruct((n_exp, n), F32)],
        compiler_params=_cparams("parallel"), name="ln1_router")(
            x, a, gi.reshape(1, d), bi.reshape(1, d), g1.reshape(1, d), b1.reshape(1, d),
            w_hi, w_lo, b_router.reshape(n_exp, 1))


def _dispatch_tables(sel_t, tok_exp_t):
    n_exp, n_tok = sel_t.shape
    nb_max = n_tok * TOP_K // MOE_TM + n_exp
    mi = (sel_t > 0.5).astype(jnp.int32)
    rank = jnp.cumsum(mi, axis=1) - mi
    counts = jnp.sum(mi, axis=1)
    nb_used = (counts + MOE_TM - 1) // MOE_TM
    n_used = jnp.sum(nb_used)
    nb = nb_used.at[n_exp - 1].add(nb_max - n_used)
    b_end = jnp.cumsum(nb)
    b_start = b_end - nb
    slot = b_start[:, None] * MOE_TM + rank
    experts = jnp.arange(n_exp, dtype=jnp.int32)
    tok_slots_t = jnp.sum(jnp.where(tok_exp_t[:, None, :] == experts[None, :, None], slot[None], 0), axis=1)
    partial = jnp.clip(b_start + nb_used - 1, 0, nb_max - 1)
    spare = jnp.minimum(n_used + experts, nb_max - 1)
    fill_blocks = jnp.concatenate([partial, spare]).astype(jnp.int32)
    return tok_slots_t.T.astype(jnp.int32), fill_blocks, b_start.astype(jnp.int32), nb.astype(jnp.int32)


def _step_tables(b_start, nb, nb_max, n_chunks):
    steps = jnp.arange(nb_max * n_chunks, dtype=jnp.int32)
    s_end = jnp.cumsum(nb) * n_chunks
    e = jnp.minimum(jnp.sum((steps[:, None] >= s_end[None, :]).astype(jnp.int32), axis=1), nb.shape[0] - 1)
    local = steps - b_start[e] * n_chunks
    nbe = jnp.maximum(nb[e], 1)
    chunk = (local // nbe).astype(jnp.int32)
    within = local % nbe
    rb = (b_start[e] + within).astype(jnp.int32)
    first = (within == 0).astype(jnp.int32)
    return rb, chunk, e, first


def _pack_bf16_pairs(x):
    half = x.shape[1] // 2
    bits = lax.bitcast_convert_type(x.astype(BF16).astype(F32), jnp.uint32)
    return (bits[:, :half] >> 16) | bits[:, half:]


def _unpack_bf16_pairs(w):
    lo = lax.bitcast_convert_type(w << 16, F32).astype(BF16)
    hi = lax.bitcast_convert_type(w & jnp.uint32(0xFFFF0000), F32).astype(BF16)
    return lo, hi


def _dispatch_rows_body(fill_ref, idx_ref, h1_ref, hp_ref, xs_hbm, pk, zero, sem, zsem):
    i = pl.program_id(0)
    n = pl.num_programs(0)
    n_rows = idx_ref.shape[2]
    buf = i % 2

    def row_copy(b, r):
        return pltpu.make_async_copy(pk.at[b, pl.ds(r // TOP_K, 1)], xs_hbm.at[pl.ds(idx_ref[0, 0, r], 1)],
                                     sem.at[b])

    def drain(b):
        def body(r, carry):
            row_copy(b, r).wait()
            return carry
        lax.fori_loop(0, n_rows, body, 0, unroll=8)

    @pl.when(i == 0)
    def _fill():
        zero[...] = jnp.zeros(zero.shape, zero.dtype)

        def body(j, carry):
            blk = pl.ds(pl.multiple_of(fill_ref[j] * MOE_TM, MOE_TM), MOE_TM)
            cp = pltpu.make_async_copy(zero, xs_hbm.at[blk], zsem.at[0])
            cp.start()
            cp.wait()
            return carry
        lax.fori_loop(0, fill_ref.shape[0], body, 0)

    @pl.when(i >= 2)
    def _reuse():
        drain(buf)

    packed = _pack_bf16_pairs(h1_ref[...])
    pk[buf] = packed
    hp_ref[...] = packed

    def issue(r, carry):
        row_copy(buf, r).start()
        return carry
    lax.fori_loop(0, n_rows, issue, 0, unroll=8)

    @pl.when(i == n - 1)
    def _tail():
        drain(1 - buf)
        drain(buf)


def _dispatch_rows(h1, tok_slots, fill_blocks, n_slots):
    n, d = h1.shape
    tt = SCATTER_TT
    nt = n // tt
    assert nt >= 2
    grid_spec = pltpu.PrefetchScalarGridSpec(
        num_scalar_prefetch=1, grid=(nt,),
        in_specs=[pl.BlockSpec((1, 1, tt * TOP_K), lambda i, f: (i, 0, 0), memory_space=pltpu.SMEM),
                  pl.BlockSpec((tt, d), lambda i, f: (i, 0))],
        out_specs=[pl.BlockSpec((tt, d // 2), lambda i, f: (i, 0)), pl.BlockSpec(memory_space=pl.ANY)],
        scratch_shapes=[pltpu.VMEM((2, tt, d // 2), jnp.uint32), pltpu.VMEM((MOE_TM, d // 2), jnp.uint32),
                        pltpu.SemaphoreType.DMA((2,)), pltpu.SemaphoreType.DMA((1,))])
    return pl.pallas_call(
        _dispatch_rows_body, grid_spec=grid_spec,
        out_shape=[jax.ShapeDtypeStruct((n, d // 2), jnp.uint32),
                   jax.ShapeDtypeStruct((n_slots, d // 2), jnp.uint32)],
        compiler_params=_cparams("arbitrary"), name="moe_dispatch")(
            fill_blocks, tok_slots.reshape(nt, 1, tt * TOP_K), h1)


def _moe_up_body(rb_ref, ch_ref, e_ref, first_ref, x_ref, wg_ref, wu_ref, hb_ref, wgb_ref, wub_ref):
    s = pl.program_id(0)

    @pl.when(first_ref[s] == 1)
    def _cast():
        wgb_ref[...] = wg_ref[0].astype(BF16)
        wub_ref[...] = wu_ref[0].astype(BF16)

    lo, hi = _unpack_bf16_pairs(x_ref[...])
    half = lo.shape[1]
    g = _dot(lo, wgb_ref[:half, :]) + _dot(hi, wgb_ref[half:, :])
    u = _dot(lo, wub_ref[:half, :]) + _dot(hi, wub_ref[half:, :])
    hb_ref[...] = (g * _sigmoid(g) * u).astype(hb_ref.dtype)


def _moe_up(xp, wg, wu, tables):
    rb, chunk, e, first = tables
    d = wg.shape[1]
    ff = wg.shape[2]
    wspec = pl.BlockSpec((1, d, MOE_TF), lambda s, rb, ch, e, f: (e[s], 0, ch[s]))
    grid_spec = pltpu.PrefetchScalarGridSpec(
        num_scalar_prefetch=4, grid=(rb.shape[0],),
        in_specs=[pl.BlockSpec((MOE_TM, d // 2), lambda s, rb, ch, e, f: (rb[s], 0)), wspec, wspec],
        out_specs=pl.BlockSpec((MOE_TM, MOE_TF), lambda s, rb, ch, e, f: (rb[s], ch[s])),
        scratch_shapes=[pltpu.VMEM((d, MOE_TF), BF16), pltpu.VMEM((d, MOE_TF), BF16)])
    return pl.pallas_call(
        _moe_up_body, grid_spec=grid_spec, out_shape=jax.ShapeDtypeStruct((xp.shape[0], ff), BF16),
        compiler_params=_cparams("arbitrary"), name="moe_up")(rb, chunk, e, first, xp, wg, wu)


def _moe_down_body(rb_ref, e_ref, first_ref, hb_ref, wd_ref, y_ref, wdb_ref):
    s = pl.program_id(0)

    @pl.when(first_ref[s] == 1)
    def _cast():
        wdb_ref[...] = wd_ref[0].astype(BF16)

    y_ref[...] = _dot(hb_ref[...], wdb_ref[...])


def _moe_down(hb, wd, tables):
    rb, _, e, first = tables
    ff = hb.shape[1]
    d = wd.shape[2]
    grid_spec = pltpu.PrefetchScalarGridSpec(
        num_scalar_prefetch=3, grid=(rb.shape[0],),
        in_specs=[pl.BlockSpec((MOE_TM, ff), lambda s, rb, e, f: (rb[s], 0)),
                  pl.BlockSpec((1, ff, d), lambda s, rb, e, f: (e[s], 0, 0))],
        out_specs=pl.BlockSpec((MOE_TM, d), lambda s, rb, e, f: (rb[s], 0)),
        scratch_shapes=[pltpu.VMEM((ff, d), BF16)])
    return pl.pallas_call(
        _moe_down_body, grid_spec=grid_spec, out_shape=jax.ShapeDtypeStruct((hb.shape[0], d), F32),
        compiler_params=_cparams("arbitrary"), name="moe_down")(rb, e, first, hb, wd)


def _swiglu_grouped(xp, wg, wu, wd, b_start, nb, nb_max):
    hb = _moe_up(xp, wg, wu, _step_tables(b_start, nb, nb_max, wg.shape[2] // MOE_TF))
    return _moe_down(hb, wd, _step_tables(b_start, nb, nb_max, 1))


def _combine_body(cur_ref, nxt_ref, y_hbm, gate_ref, h1_ref, ysh_ref, g_ref, b_ref, o_ref, gbuf, sem):
    i = pl.program_id(0)
    n = pl.num_programs(0)
    n_rows = cur_ref.shape[2]

    def fetch(idx_ref, buf, r):
        return pltpu.make_async_copy(y_hbm.at[pl.ds(idx_ref[0, 0, r], 1)],
                                     gbuf.at[buf, r % TOP_K, pl.ds(r // TOP_K, 1)], sem.at[buf])

    def issue(idx_ref, buf):
        def body(r, carry):
            fetch(idx_ref, buf, r).start()
            return carry
        lax.fori_loop(0, n_rows, body, 0, unroll=8)

    @pl.when(i == 0)
    def _first():
        issue(cur_ref, 0)

    @pl.when(i + 1 < n)
    def _ahead():
        issue(nxt_ref, (i + 1) % 2)

    buf = i % 2

    def drain(r, carry):
        fetch(cur_ref, buf, r).wait()
        return carry

    lax.fori_loop(0, n_rows, drain, 0, unroll=8)
    gate = gate_ref[...]
    routed = gbuf[buf, 0] * gate[:, 0:1]
    for k in range(1, TOP_K):
        routed = routed + gbuf[buf, k] * gate[:, k:k + 1]
    o_ref[...] = _layer_norm(ALPHA * h1_ref[...] + (routed + ysh_ref[...]), g_ref[...], b_ref[...])


def _combine_ln2(y, tok_slots, tok_gates, h1, ysh, g, b):
    n, d = h1.shape
    tt = COMBINE_TT
    nt = n // tt
    idx = tok_slots.reshape(nt, 1, tt * TOP_K)
    row = pl.BlockSpec((tt, d), lambda i: (i, 0))
    vec = pl.BlockSpec((1, d), lambda i: (0, 0))
    return pl.pallas_call(
        _combine_body, grid=(nt,),
        in_specs=[pl.BlockSpec((1, 1, tt * TOP_K), lambda i: (i, 0, 0), memory_space=pltpu.SMEM),
                  pl.BlockSpec((1, 1, tt * TOP_K), lambda i: (jnp.minimum(i + 1, nt - 1), 0, 0),
                               memory_space=pltpu.SMEM),
                  pl.BlockSpec(memory_space=pl.ANY), pl.BlockSpec((tt, TOP_K), lambda i: (i, 0)),
                  row, row, vec, vec],
        out_specs=row, out_shape=jax.ShapeDtypeStruct((n, d), F32),
        scratch_shapes=[pltpu.VMEM((2, TOP_K, tt, d), F32), pltpu.SemaphoreType.DMA((2,))],
        compiler_params=_cparams("arbitrary"), name="moe_combine_ln2")(
            idx, idx, y, tok_gates, h1, ysh, g.reshape(1, d), b.reshape(1, d))


def _forward(x_prompt, x_sample, cache_moba_k, cache_moba_v, cache_diff_k, cache_diff_v, page_table,
             ln_in_g, ln_in_b, w_in, diff_lq1, diff_lk1, diff_lq2, diff_lk2, diff_norm_g, w_out,
             ln1_g, ln1_b, w_router, b_router, w_exp_gate, w_exp_up, w_exp_down,
             w_sh_gate, w_sh_up, w_sh_down, ln2_g, ln2_b):
    bsz, s_len, d = x_prompt.shape
    dbsz, t_len, _ = x_sample.shape
    n_p = bsz * s_len
    n_s = dbsz * t_len
    n = n_p + n_s
    mw = d // 2
    m_heads = mw // HEAD_DIM
    d_heads = mw // (2 * HEAD_DIM)
    past = page_table.shape[1] * cache_moba_k.shape[2]
    lyr = 0

    x = jnp.concatenate([x_prompt.reshape(n_p, d), x_sample.reshape(n_s, d)], axis=0)
    pos = jnp.concatenate([jnp.tile(jnp.arange(s_len, dtype=jnp.int32), bsz),
                           jnp.tile(past + jnp.arange(t_len, dtype=jnp.int32), dbsz)])
    tabs = _rope_tables(pos)
    h = _ln_in(x, ln_in_g, ln_in_b)
    w_in_b = w_in[lyr].astype(BF16)
    qscale = 1.0 / math.sqrt(HEAD_DIM)
    (qm,) = _proj(h, w_in_b, 0, mw, (BF16,), tabs, qscale, "proj_qm")
    km, km_b = _proj(h, w_in_b, mw, mw, (F32, BF16), tabs, 1.0, "proj_km")
    vm, vm_b = _proj(h, w_in_b, 2 * mw, mw, (F32, BF16), None, 1.0, "proj_vm")
    (qd,) = _proj(h, w_in_b, 3 * mw, mw, (BF16,), tabs, qscale, "proj_qd")
    kd, kd_b = _proj(h, w_in_b, 4 * mw, mw, (F32, BF16), tabs, 1.0, "proj_kd")
    vd, vd_b = _proj(h, w_in_b, 5 * mw, mw, (F32, BF16), None, 1.0, "proj_vd")

    lams = [v[lyr].reshape(1, HEAD_DIM) for v in (diff_lq1, diff_lk1, diff_lq2, diff_lk2)]
    g_diff = diff_norm_g[lyr].reshape(1, 2 * HEAD_DIM)

    nbp = s_len // MOBA_BLOCK
    means = _moba_means(km, n_p).reshape(bsz, nbp, mw)
    means = jnp.pad(means, ((0, 0), (0, LANES - nbp), (0, 0)))
    om_p = _moba_prompt(qm, km_b, vm_b, means, bsz, s_len)
    od_p = _diff_prompt(qd, kd_b, vd_b, g_diff, lams, bsz, s_len)

    def smp(a, w):
        return a[n_p:].reshape(dbsz, t_len * (mw // w), w)

    hw = 2 * HEAD_DIM
    om_s, od_s = _sample_attn(page_table, smp(qm, HEAD_DIM), smp(qd, hw), smp(km, HEAD_DIM), smp(vm, HEAD_DIM),
                              smp(kd, hw), smp(vd, hw), cache_moba_k, cache_moba_v, cache_diff_k,
                              cache_diff_v, g_diff, lams, lyr, t_len)
    o_s = jnp.concatenate([om_s.reshape(n_s, mw), od_s.reshape(n_s, mw)], axis=1)

    o_all = jnp.concatenate([jnp.concatenate([om_p, od_p], axis=1), o_s.astype(BF16)], axis=0)
    (a,) = _proj(o_all, w_out[lyr].astype(BF16), 0, d, (F32,), None, 1.0, "proj_out")
    h1, tok_exp_t, tok_gate_t, sel_t = _ln1_router(x, a, ln_in_g, ln_in_b, ln1_g[lyr], ln1_b[lyr],
                                                   w_router[lyr], b_router[lyr])

    tok_slots, fill_blocks, b_start, nb = _dispatch_tables(sel_t, tok_exp_t)
    nb_max = n * TOP_K // MOE_TM + sel_t.shape[0]
    h1p, xs = _dispatch_rows(h1, tok_slots, fill_blocks, nb_max * MOE_TM)
    y = _swiglu_grouped(xs, w_exp_gate[lyr], w_exp_up[lyr], w_exp_down[lyr], b_start, nb, nb_max)
    nb_sh = n // MOE_TM
    ysh = _swiglu_grouped(h1p, w_sh_gate[lyr][None], w_sh_up[lyr][None], w_sh_down[lyr][None],
                          jnp.zeros((1,), jnp.int32), jnp.full((1,), nb_sh, jnp.int32), nb_sh)
    tok_gates = tok_gate_t.T
    out = _combine_ln2(y, tok_slots, tok_gates, h1, ysh, ln2_g[lyr], ln2_b[lyr])

    def prm(t, nh):
        return t[:n_p].reshape(1, bsz, s_len, nh, mw // nh)

    def dec(t, nh):
        return t[n_p:].reshape(1, dbsz, t_len, nh, mw // nh)

    outs = (out[:n_p].reshape(bsz, s_len, d), out[n_p:].reshape(dbsz, t_len, d),
            prm(km, m_heads), prm(vm, m_heads), prm(kd, d_heads), prm(vd, d_heads),
            dec(km, m_heads), dec(vm, m_heads), dec(kd, d_heads), dec(vd, d_heads))
    mid = dict(a=a, h1=h1, y=y, tok_slots=tok_slots, tok_gates=tok_gates, ysh=ysh)
    return outs, mid


def kernel(x_prompt, x_sample, cache_moba_k, cache_moba_v, cache_diff_k, cache_diff_v, page_table, ln_in_g, ln_in_b, w_in, diff_lq1, diff_lk1, diff_lq2, diff_lk2, diff_norm_g, w_out, ln1_g, ln1_b, w_router, b_router, w_exp_gate, w_exp_up, w_exp_down, w_sh_gate, w_sh_up, w_sh_down, ln2_g, ln2_b):
    outs, _ = _forward(x_prompt, x_sample, cache_moba_k, cache_moba_v, cache_diff_k, cache_diff_v, page_table,
                       ln_in_g, ln_in_b, w_in, diff_lq1, diff_lk1, diff_lq2, diff_lk2, diff_norm_g, w_out,
                       ln1_g, ln1_b, w_router, b_router, w_exp_gate, w_exp_up, w_exp_down,
                       w_sh_gate, w_sh_up, w_sh_down, ln2_g, ln2_b)
    return outs
```

```python
import functools
import math

import jax
import jax.numpy as jnp
from jax import lax
from jax.experimental import pallas as pl
from jax.experimental.pallas import tpu as pltpu

F32 = jnp.float32
BF16 = jnp.bfloat16

HEAD_DIM = 128
MOBA_BLOCK = 256
MOBA_TOPK = 3
ROPE_THETA = 500000.0
ROPE_DIV = 4
N_GROUPS = 8
TOPK_GROUPS = 4
TOP_K = 8
ROUTED_SCALE = 2.5
LN_EPS = 1e-5
NEG = -1e30
DEPTH = 1
ALPHA = (2.0 * DEPTH) ** 0.25
LAM_INIT = 0.8 - 0.6 * math.exp(-0.3 * 0)

VMEM_LIMIT_BYTES = 56 * 1024 * 1024
LANES = 128

ROW_TILE = 256
MM_TM = 512
MM_TN = 512
ATT_TQ = MOBA_BLOCK
MOBA_HEADS_PER_STEP = 4
DIFF_HEADS_PER_STEP = 2
MOE_TM = 256
MOE_TF = 512
SCATTER_TT = 64
COMBINE_TT = 32


def _cparams(*sem):
    return pltpu.CompilerParams(dimension_semantics=sem, vmem_limit_bytes=VMEM_LIMIT_BYTES)


def _dot(a, b):
    return jnp.dot(a, b, preferred_element_type=F32)


def _dot_nt(a, b):
    return lax.dot_general(a, b, (((1,), (1,)), ((), ())), preferred_element_type=F32)


def _layer_norm(x, g, b):
    mu = jnp.mean(x, axis=-1, keepdims=True)
    xc = x - mu
    var = jnp.mean(xc * xc, axis=-1, keepdims=True)
    return xc * lax.rsqrt(var + LN_EPS) * g + b


def _sigmoid(x):
    return 1.0 / (1.0 + jnp.exp(-x))


def _ln_in_body(x_ref, g_ref, b_ref, o_ref):
    o_ref[...] = _layer_norm(x_ref[...], g_ref[...], b_ref[...]).astype(o_ref.dtype)


def _ln_in(x, g, b):
    n, d = x.shape
    row = pl.BlockSpec((ROW_TILE, d), lambda i: (i, 0))
    vec = pl.BlockSpec((1, d), lambda i: (0, 0))
    return pl.pallas_call(
        _ln_in_body, grid=(n // ROW_TILE,), in_specs=[row, vec, vec], out_specs=row,
        out_shape=jax.ShapeDtypeStruct((n, d), BF16), compiler_params=_cparams("parallel"),
        name="ln_in")(x, g.reshape(1, d), b.reshape(1, d))


def _rope_chunk(y, c, sa, sb):
    return y * c + pltpu.roll(y, HEAD_DIM - 16, 1) * sa + pltpu.roll(y, 16, 1) * sb


def _proj_body(*refs, rope, scale, transposed):
    if rope:
        h_ref, w_ref, c_ref, sa_ref, sb_ref = refs[:5]
    else:
        h_ref, w_ref = refs[:2]
    out_refs = refs[len(refs) - len(transposed):]
    acc = _dot(h_ref[...], w_ref[...])
    if rope:
        c, sa, sb = c_ref[...], sa_ref[...], sb_ref[...]
        acc = jnp.concatenate(
            [_rope_chunk(acc[:, k * HEAD_DIM:(k + 1) * HEAD_DIM], c, sa, sb)
             for k in range(acc.shape[1] // HEAD_DIM)], axis=1)
    if scale != 1.0:
        acc = acc * scale
    acc_t = acc.T if any(transposed) else None
    for o_ref, tr in zip(out_refs, transposed):
        o_ref[...] = (acc_t if tr else acc).astype(o_ref.dtype)


def _proj(h, w, col0, ncols, outs, rope_tabs=None, scale=1.0, name="proj"):
    n, k = h.shape
    tm, tn = MM_TM, MM_TN
    cb = col0 // tn
    in_specs = [pl.BlockSpec((tm, k), lambda i, j: (i, 0)),
                pl.BlockSpec((k, tn), lambda i, j: (0, cb + j))]
    args = [h, w]
    if rope_tabs is not None:
        in_specs += [pl.BlockSpec((tm, HEAD_DIM), lambda i, j: (i, 0))] * 3
        args += list(rope_tabs)
    out_specs = [pl.BlockSpec((tn, tm), lambda i, j: (j, i)) if tr else pl.BlockSpec((tm, tn), lambda i, j: (i, j))
                 for _, tr in outs]
    out_shape = [jax.ShapeDtypeStruct((ncols, n) if tr else (n, ncols), dt) for dt, tr in outs]
    return pl.pallas_call(
        functools.partial(_proj_body, rope=rope_tabs is not None, scale=scale,
                          transposed=tuple(tr for _, tr in outs)),
        grid=(n // tm, ncols // tn), in_specs=in_specs, out_specs=out_specs, out_shape=out_shape,
        compiler_params=_cparams("parallel", "arbitrary"), name=name)(*args)


def _rope_tables(pos):
    rot = HEAD_DIM // ROPE_DIV
    half = rot // 2
    inv_freq = ROPE_THETA ** (-jnp.arange(half, dtype=F32) * (2.0 / rot))
    ang = pos.astype(F32)[:, None] * inv_freq[None, :]
    cos, sin = jnp.cos(ang), jnp.sin(ang)
    n = pos.shape[0]
    ones = jnp.ones((n, HEAD_DIM - rot), F32)
    zeros = jnp.zeros((n, HEAD_DIM - rot), F32)
    zh = jnp.zeros((n, half), F32)
    c = jnp.concatenate([cos, cos, ones], axis=1)
    sa = jnp.concatenate([-sin, zh, zeros], axis=1)
    sb = jnp.concatenate([zh, sin, zeros], axis=1)
    return c, sa, sb


def _means_body(k_ref, o_ref):
    o_ref[0] = jnp.mean(k_ref[...], axis=0, keepdims=True)


def _moba_means(k, n_rows):
    w = k.shape[1]
    nb = n_rows // MOBA_BLOCK
    return pl.pallas_call(
        _means_body, grid=(nb,), in_specs=[pl.BlockSpec((MOBA_BLOCK, w), lambda i: (i, 0))],
        out_specs=pl.BlockSpec((1, 1, w), lambda i: (i, 0, 0)),
        out_shape=jax.ShapeDtypeStruct((nb, 1, w), F32), compiler_params=_cparams("parallel"),
        name="moba_means")(k)


def _pick_top(cur, idx_f, n_pick, axis, big):
    sel = jnp.zeros(cur.shape, jnp.bool_)
    for _ in range(n_pick):
        mx = jnp.max(cur, axis=axis, keepdims=True)
        first = jnp.min(jnp.where(cur == mx, idx_f, big), axis=axis, keepdims=True)
        pick = idx_f == first
        sel = jnp.logical_or(sel, pick)
        cur = jnp.where(pick, -jnp.inf, cur)
    return sel


def _row_reduce(x, op, reduce):
    if x.shape[1] % LANES:
        return reduce(x, axis=1, keepdims=True)
    folded = x[:, :LANES]
    for j in range(1, x.shape[1] // LANES):
        folded = op(folded, x[:, j * LANES:(j + 1) * LANES])
    return reduce(folded, axis=1, keepdims=True)


def _softmax_step_t(st, vt, m_ref, l_ref, acc_ref, c, first):
    if first:
        m_new = jnp.max(st, axis=0, keepdims=True)
        p = jnp.exp(st - m_new)
        l_ref[c] = jnp.sum(p, axis=0, keepdims=True)
        acc_ref[c] = _dot(vt, p.astype(BF16))
    else:
        m_old = m_ref[c]
        m_new = jnp.maximum(m_old, jnp.max(st, axis=0, keepdims=True))
        alpha = jnp.exp(m_old - m_new)
        p = jnp.exp(st - m_new)
        l_ref[c] = alpha * l_ref[c] + jnp.sum(p, axis=0, keepdims=True)
        acc_ref[c] = alpha * acc_ref[c] + _dot(vt, p.astype(BF16))
    m_ref[c] = m_new


def _moba_prompt_body(qt_ref, k_ref, vt_ref, mean_ref, o_ref, sel_ref, m_ref, l_ref, acc_ref):
    qi = pl.program_id(2)
    tq = qt_ref.shape[1]
    n_h = qt_ref.shape[0] // HEAD_DIM
    n_blk = mean_ref.shape[1]
    blk_id = lax.broadcasted_iota(jnp.int32, (n_blk, tq), 0)
    past = blk_id < qi
    key = lax.broadcasted_iota(jnp.int32, (tq, tq), 0)
    qry = lax.broadcasted_iota(jnp.int32, (tq, tq), 1)
    own = pl.ds(pl.multiple_of(qi * tq, tq), tq)
    hs = [slice(h * HEAD_DIM, (h + 1) * HEAD_DIM) for h in range(n_h)]
    for h in range(n_h):
        qt = qt_ref[hs[h], :]
        mean = mean_ref[0, :, hs[h]]
        m_hi = mean.astype(BF16)
        m_lo = (mean - m_hi.astype(F32)).astype(BF16)
        gate = _dot(m_hi, qt) + _dot(m_lo, qt)
        sel = _pick_top(jnp.where(past, gate, -jnp.inf), blk_id.astype(F32), MOBA_TOPK, 0, float(n_blk))
        sel_ref[h] = jnp.where(jnp.logical_and(sel, past), 1.0, 0.0)
        st = jnp.where(key <= qry, _dot(k_ref[own, hs[h]], qt), NEG)
        _softmax_step_t(st, vt_ref[hs[h], own], m_ref, l_ref, acc_ref, h, True)

    def body(ki, carry):
        blk = pl.ds(pl.multiple_of(ki * tq, tq), tq)
        picked = [sel_ref[h, pl.ds(ki, 1), :] > 0.5 for h in range(n_h)]
        for h in range(n_h):
            st = jnp.where(picked[h], _dot(k_ref[blk, hs[h]], qt_ref[hs[h], :]), NEG)
            _softmax_step_t(st, vt_ref[hs[h], blk], m_ref, l_ref, acc_ref, h, False)
        return carry

    lax.fori_loop(0, qi, body, 0)
    for h in range(n_h):
        o_ref[:, hs[h]] = (acc_ref[h] / l_ref[h]).T.astype(o_ref.dtype)


def _moba_prompt(qt, k, vt, means, bsz, s_len):
    hb = MOBA_HEADS_PER_STEP
    w = hb * HEAD_DIM
    groups = qt.shape[0] // w
    nq = s_len // ATT_TQ
    return pl.pallas_call(
        _moba_prompt_body, grid=(bsz, groups, nq),
        in_specs=[pl.BlockSpec((w, ATT_TQ), lambda b, h, i: (h, b * nq + i)),
                  pl.BlockSpec((s_len, w), lambda b, h, i: (b, h)),
                  pl.BlockSpec((w, s_len), lambda b, h, i: (h, b)),
                  pl.BlockSpec((1, means.shape[1], w), lambda b, h, i: (b, 0, h))],
        out_specs=pl.BlockSpec((ATT_TQ, w), lambda b, h, i: (b * nq + i, h)),
        out_shape=jax.ShapeDtypeStruct((bsz * s_len, qt.shape[0]), BF16),
        scratch_shapes=[pltpu.VMEM((hb, means.shape[1], ATT_TQ), F32), pltpu.VMEM((hb, 1, ATT_TQ), F32),
                        pltpu.VMEM((hb, 1, ATT_TQ), F32), pltpu.VMEM((hb, HEAD_DIM, ATT_TQ), F32)],
        compiler_params=_cparams("parallel", "parallel", "arbitrary"), name="moba_prompt")(qt, k, vt, means)


def _lambda(lq1, lk1, lq2, lk2):
    return (jnp.exp(jnp.sum(lq1 * lk1, axis=1, keepdims=True))
            - jnp.exp(jnp.sum(lq2 * lk2, axis=1, keepdims=True)) + LAM_INIT)


def _diff_out(acc0, l0, acc1, l1, lam, g):
    o = acc0 / l0 - lam * (acc1 / l1)
    ms = jnp.mean(o * o, axis=1, keepdims=True)
    return o * lax.rsqrt(ms + LN_EPS) * g * (1.0 - LAM_INIT)


def _diff_prompt_body(qt_ref, k_ref, vt_ref, gt_ref, lq1, lk1, lq2, lk2, o_ref, m_ref, l_ref, acc_ref):
    qi = pl.program_id(2)
    tq = qt_ref.shape[1]
    hw = 2 * HEAD_DIM
    n_h = qt_ref.shape[0] // hw
    key = lax.broadcasted_iota(jnp.int32, (tq, tq), 0)
    qry = lax.broadcasted_iota(jnp.int32, (tq, tq), 1)
    own = pl.ds(pl.multiple_of(qi * tq, tq), tq)

    def comp(h, c):
        return slice(h * hw + c * HEAD_DIM, h * hw + (c + 1) * HEAD_DIM)

    def head(h):
        return slice(h * hw, (h + 1) * hw)

    for h in range(n_h):
        for c in range(2):
            st = jnp.where(key <= qry, _dot(k_ref[own, comp(h, c)], qt_ref[comp(h, c), :]), NEG)
            _softmax_step_t(st, vt_ref[head(h), own], m_ref, l_ref, acc_ref, 2 * h + c, True)

    def body(ki, carry):
        blk = pl.ds(pl.multiple_of(ki * tq, tq), tq)
        for h in range(n_h):
            for c in range(2):
                st = _dot(k_ref[blk, comp(h, c)], qt_ref[comp(h, c), :])
                _softmax_step_t(st, vt_ref[head(h), blk], m_ref, l_ref, acc_ref, 2 * h + c, False)
        return carry

    lax.fori_loop(0, qi, body, 0)
    lam = _lambda(lq1[...], lk1[...], lq2[...], lk2[...])
    for h in range(n_h):
        ot = acc_ref[2 * h] / l_ref[2 * h] - lam * (acc_ref[2 * h + 1] / l_ref[2 * h + 1])
        ms = jnp.mean(ot * ot, axis=0, keepdims=True)
        ot = ot * lax.rsqrt(ms + LN_EPS) * gt_ref[...] * (1.0 - LAM_INIT)
        o_ref[:, head(h)] = ot.T.astype(o_ref.dtype)


def _diff_prompt(qt, k, vt, g, lams, bsz, s_len):
    hw = 2 * HEAD_DIM
    hb = DIFF_HEADS_PER_STEP
    w = hb * hw
    groups = qt.shape[0] // w
    nq = s_len // ATT_TQ
    vec = pl.BlockSpec((1, HEAD_DIM), lambda b, h, i: (0, 0))
    return pl.pallas_call(
        _diff_prompt_body, grid=(bsz, groups, nq),
        in_specs=[pl.BlockSpec((w, ATT_TQ), lambda b, h, i: (h, b * nq + i)),
                  pl.BlockSpec((s_len, w), lambda b, h, i: (b, h)),
                  pl.BlockSpec((w, s_len), lambda b, h, i: (h, b)),
                  pl.BlockSpec((hw, 1), lambda b, h, i: (0, 0)), vec, vec, vec, vec],
        out_specs=pl.BlockSpec((ATT_TQ, w), lambda b, h, i: (b * nq + i, h)),
        out_shape=jax.ShapeDtypeStruct((bsz * s_len, qt.shape[0]), BF16),
        scratch_shapes=[pltpu.VMEM((2 * hb, 1, ATT_TQ), F32), pltpu.VMEM((2 * hb, 1, ATT_TQ), F32),
                        pltpu.VMEM((2 * hb, hw, ATT_TQ), F32)],
        compiler_params=_cparams("parallel", "parallel", "arbitrary"), name="diff_prompt")(
            qt, k, vt, g.reshape(hw, 1), *lams)


def _sample_attn_body(pt_ref, qm_ref, qd_ref, kmn_ref, vmn_ref, kdn_ref, vdn_ref,
                      mk0, mk1, mv0, mv1, dk0, dk1, dv0, dv1, g_ref, lq1, lk1, lq2, lk2,
                      om_ref, od_ref, gate_ref, mall_ref, lall_ref, oblk_ref, md_ref, ld_ref, accd_ref,
                      *, t_len):
    del pt_ref
    n = pl.program_id(1)
    n_blk = pl.num_programs(1)
    page, m_heads, _ = mk0.shape[2:]
    d_heads, hw = dk0.shape[3:]
    qm = qm_ref[0]
    qd = qd_ref[0]
    rm, rd = qm.shape[0], qd.shape[0]
    lane = lax.broadcasted_iota(jnp.int32, (rm, LANES), 1)

    def head_match(rows, cols, heads):
        r = lax.broadcasted_iota(jnp.int32, (rows, cols), 0)
        c = lax.broadcasted_iota(jnp.int32, (rows, cols), 1)
        return r, c, (r % heads) == (c % heads)

    def rows2d(ref):
        x = ref[0, 0]
        return x.reshape(x.shape[0] * x.shape[1], x.shape[2])

    @pl.when(n == 0)
    def _init():
        gate_ref[...] = jnp.zeros(gate_ref.shape, F32)
        mall_ref[...] = jnp.zeros(mall_ref.shape, F32)
        lall_ref[...] = jnp.zeros(lall_ref.shape, F32)
        md_ref[...] = jnp.full(md_ref.shape, NEG, F32)
        ld_ref[...] = jnp.zeros(ld_ref.shape, F32)
        accd_ref[...] = jnp.zeros(accd_ref.shape, F32)

    k0, k1 = mk0[0, 0], mk1[0, 0]
    mean = (jnp.sum(k0, axis=0) + jnp.sum(k1, axis=0)) * (1.0 / (2 * page))
    mean_rows = jnp.concatenate([mean] * t_len, axis=0)
    gate_n = jnp.sum(qm.astype(F32) * mean_rows, axis=1, keepdims=True)
    kb = jnp.concatenate([rows2d(mk0), rows2d(mk1)], axis=0).astype(BF16)
    vb = jnp.concatenate([rows2d(mv0), rows2d(mv1)], axis=0).astype(BF16)
    _, _, same = head_match(rm, kb.shape[0], m_heads)
    s = jnp.where(same, _dot_nt(qm, kb), NEG)
    m_n = _row_reduce(s, jnp.maximum, jnp.max)
    p = jnp.exp(s - m_n)
    l_n = _row_reduce(p, jnp.add, jnp.sum)
    oblk_ref[n] = _dot(p.astype(BF16), vb)
    here = lane == n
    gate_ref[...] = jnp.where(here, gate_n, gate_ref[...])
    mall_ref[...] = jnp.where(here, m_n, mall_ref[...])
    lall_ref[...] = jnp.where(here, l_n, lall_ref[...])

    kd = jnp.concatenate([rows2d(dk0), rows2d(dk1)], axis=0).astype(BF16)
    vd = jnp.concatenate([rows2d(dv0), rows2d(dv1)], axis=0).astype(BF16)
    _, _, same_d = head_match(rd, kd.shape[0], d_heads)

    def diff_update(c, s, v):
        m_old = md_ref[c]
        m_new = jnp.maximum(m_old, _row_reduce(s, jnp.maximum, jnp.max))
        alpha = jnp.exp(m_old - m_new)
        p = jnp.exp(s - m_new)
        ld_ref[c] = alpha * ld_ref[c] + _row_reduce(p, jnp.add, jnp.sum)
        accd_ref[c] = alpha * accd_ref[c] + _dot(p.astype(BF16), v)
        md_ref[c] = m_new

    for c in range(2):
        cs = slice(c * HEAD_DIM, (c + 1) * HEAD_DIM)
        diff_update(c, jnp.where(same_d, _dot_nt(qd[:, cs], kd[:, cs]), NEG), vd)

    @pl.when(n == n_blk - 1)
    def _finish():
        r, c, same = head_match(rm, rm, m_heads)
        own_ok = jnp.logical_and(same, c // m_heads <= r // m_heads)
        s_own = jnp.where(own_ok, _dot_nt(qm, kmn_ref[0].astype(BF16)), NEG)
        cached = lane < n_blk
        sel = _pick_top(jnp.where(cached, gate_ref[...], -jnp.inf), lane.astype(F32), MOBA_TOPK, 1,
                        float(LANES))
        sel = jnp.logical_and(sel, cached)
        mall = mall_ref[...]
        m_tot = jnp.maximum(jnp.max(jnp.where(sel, mall, NEG), axis=1, keepdims=True),
                            jnp.max(s_own, axis=1, keepdims=True))
        w_blk = jnp.where(sel, jnp.exp(mall - m_tot), 0.0)
        p_own = jnp.exp(s_own - m_tot)
        denom = (jnp.sum(w_blk * lall_ref[...], axis=1, keepdims=True)
                 + jnp.sum(p_own, axis=1, keepdims=True))
        o_m = _dot(p_own.astype(BF16), vmn_ref[0].astype(BF16))
        for b in range(oblk_ref.shape[0]):
            o_m = o_m + w_blk[:, b:b + 1] * oblk_ref[b]
        om_ref[0] = o_m / denom

        r, c, same = head_match(rd, rd, d_heads)
        own_ok = jnp.logical_and(same, c // d_heads <= r // d_heads)
        kdn = kdn_ref[0].astype(BF16)
        vdn = vdn_ref[0].astype(BF16)
        for comp in range(2):
            cs = slice(comp * HEAD_DIM, (comp + 1) * HEAD_DIM)
            diff_update(comp, jnp.where(own_ok, _dot_nt(qd[:, cs], kdn[:, cs]), NEG), vdn)
        lam = _lambda(lq1[...], lk1[...], lq2[...], lk2[...])
        od_ref[0] = _diff_out(accd_ref[0], ld_ref[0], accd_ref[1], ld_ref[1], lam, g_ref[...])


def _sample_attn(page_table, qm, qd, kmn, vmn, kdn, vdn, cmk, cmv, cdk, cdv, g, lams, lyr, t_len):
    dbsz, rm, _ = qm.shape
    rd, hw = qd.shape[1:]
    n_pages = page_table.shape[1]
    page, m_heads = cmk.shape[2:4]
    d_heads = cdk.shape[3]
    ppb = MOBA_BLOCK // page
    assert ppb == 2 and (n_pages * page) % MOBA_BLOCK == 0 and t_len <= MOBA_BLOCK
    n_blk = n_pages // ppb
    assert MOBA_TOPK <= n_blk <= LANES

    def tok_spec(rows, w):
        return pl.BlockSpec((1, rows, w), lambda s, n, pt: (s, 0, 0))

    def page_spec(j, heads, w):
        return pl.BlockSpec((1, 1, page, heads, w),
                            lambda s, n, pt: (lyr, pt[s * n_pages + ppb * n + j], 0, 0, 0))

    mp = [page_spec(0, m_heads, HEAD_DIM), page_spec(1, m_heads, HEAD_DIM)]
    dp = [page_spec(0, d_heads, hw), page_spec(1, d_heads, hw)]
    vec = pl.BlockSpec((1, HEAD_DIM), lambda s, n, pt: (0, 0))
    grid_spec = pltpu.PrefetchScalarGridSpec(
        num_scalar_prefetch=1, grid=(dbsz, n_blk),
        in_specs=[tok_spec(rm, HEAD_DIM), tok_spec(rd, hw), tok_spec(rm, HEAD_DIM), tok_spec(rm, HEAD_DIM),
                  tok_spec(rd, hw), tok_spec(rd, hw)] + mp + mp + dp + dp
        + [pl.BlockSpec((1, hw), lambda s, n, pt: (0, 0)), vec, vec, vec, vec],
        out_specs=[tok_spec(rm, HEAD_DIM), tok_spec(rd, hw)],
        scratch_shapes=[pltpu.VMEM((rm, LANES), F32), pltpu.VMEM((rm, LANES), F32),
                        pltpu.VMEM((rm, LANES), F32), pltpu.VMEM((n_blk, rm, HEAD_DIM), F32),
                        pltpu.VMEM((2, rd, 1), F32), pltpu.VMEM((2, rd, 1), F32),
                        pltpu.VMEM((2, rd, hw), F32)])
    return pl.pallas_call(
        functools.partial(_sample_attn_body, t_len=t_len), grid_spec=grid_spec,
        out_shape=[jax.ShapeDtypeStruct((dbsz, rm, HEAD_DIM), F32), jax.ShapeDtypeStruct((dbsz, rd, hw), F32)],
        compiler_params=_cparams("parallel", "arbitrary"), name="sample_attn")(
            page_table.reshape(-1), qm, qd, kmn, vmn, kdn, vdn,
            cmk, cmk, cmv, cmv, cdk, cdk, cdv, cdv, g, *lams)


def _ln1_router_body(x_ref, a_ref, gi_ref, bi_ref, g1_ref, b1_ref, wh_ref, wl_ref, br_ref,
                     h1_ref, exp_ref, gate_ref, sel_ref):
    h = _layer_norm(x_ref[...], gi_ref[...], bi_ref[...])
    z = _layer_norm(ALPHA * h + a_ref[...], g1_ref[...], b1_ref[...])
    h1_ref[...] = z
    z_hi = z.astype(BF16)
    z_lo = (z - z_hi.astype(F32)).astype(BF16)
    w_hi = wh_ref[...]
    logits = _dot_nt(w_hi, z_hi) + _dot_nt(w_hi, z_lo) + _dot_nt(wl_ref[...], z_hi)
    scores = _sigmoid(logits)
    choice = scores + br_ref[...]
    n_exp, tm = choice.shape
    per = n_exp // N_GROUPS
    c3 = choice.reshape(N_GROUPS, per, tm)
    sub = lax.broadcasted_iota(jnp.int32, c3.shape, 1).astype(F32)
    top2 = _pick_top(c3, sub, 2, 1, float(per))
    grp = jnp.sum(jnp.where(top2, c3, 0.0), axis=1)
    gidx = lax.broadcasted_iota(jnp.int32, grp.shape, 0).astype(F32)
    keep_g = _pick_top(grp, gidx, TOPK_GROUPS, 0, float(N_GROUPS))
    keep = jnp.broadcast_to(keep_g[:, None, :], c3.shape).reshape(n_exp, tm)
    eidx = lax.broadcasted_iota(jnp.int32, choice.shape, 0).astype(F32)
    cur = jnp.where(keep, choice, NEG)
    sel = jnp.zeros(cur.shape, jnp.bool_)
    ids, ws = [], []
    for _ in range(TOP_K):
        mx = jnp.max(cur, axis=0, keepdims=True)
        first = jnp.min(jnp.where(cur == mx, eidx, float(n_exp)), axis=0, keepdims=True)
        pick = eidx == first
        sel = jnp.logical_or(sel, pick)
        ids.append(first)
        ws.append(jnp.sum(jnp.where(pick, scores, 0.0), axis=0, keepdims=True))
        cur = jnp.where(pick, -jnp.inf, cur)
    wsel = jnp.concatenate(ws, axis=0)
    exp_ref[...] = jnp.concatenate(ids, axis=0).astype(jnp.int32)
    gate_ref[...] = wsel / jnp.sum(wsel, axis=0, keepdims=True) * ROUTED_SCALE
    sel_ref[...] = jnp.where(sel, 1.0, 0.0)


def _ln1_router(x, a, gi, bi, g1, b1, w_router, b_router):
    n, d = x.shape
    n_exp = w_router.shape[1]
    wt = w_router.T
    w_hi = wt.astype(BF16)
    w_lo = (wt - w_hi.astype(F32)).astype(BF16)
    row = pl.BlockSpec((ROW_TILE, d), lambda i: (i, 0))
    vec = pl.BlockSpec((1, d), lambda i: (0, 0))
    wspec = pl.BlockSpec((n_exp, d), lambda i: (0, 0))
    ex = pl.BlockSpec((n_exp, ROW_TILE), lambda i: (0, i))
    tk = pl.BlockSpec((TOP_K, ROW_TILE), lambda i: (0, i))
    return pl.pallas_call(
        _ln1_router_body, grid=(n // ROW_TILE,),
        in_specs=[row, row, vec, vec, vec, vec, wspec, wspec, pl.BlockSpec((n_exp, 1), lambda i: (0, 0))],
        out_specs=[row, tk, tk, ex],
        out_shape=[jax.ShapeDtypeStruct((n, d), F32), jax.ShapeDtypeStruct((TOP_K, n), jnp.int32),
                   jax.ShapeDtypeStruct((TOP_K, n), F32), jax.ShapeDtypeStruct((n_exp, n), F32)],
        compiler_params=_cparams("parallel"), name="ln1_router")(
            x, a, gi.reshape(1, d), bi.reshape(1, d), g1.reshape(1, d), b1.reshape(1, d),
            w_hi, w_lo, b_router.reshape(n_exp, 1))


def _dispatch_tables(sel_t, tok_exp_t):
    n_exp, n_tok = sel_t.shape
    nb_max = n_tok * TOP_K // MOE_TM + n_exp
    mi = (sel_t > 0.5).astype(jnp.int32)
    rank = jnp.cumsum(mi, axis=1) - mi
    counts = jnp.sum(mi, axis=1)
    nb_used = (counts + MOE_TM - 1) // MOE_TM
    n_used = jnp.sum(nb_used)
    nb = nb_used.at[n_exp - 1].add(nb_max - n_used)
    b_end = jnp.cumsum(nb)
    b_start = b_end - nb
    slot = b_start[:, None] * MOE_TM + rank
    experts = jnp.arange(n_exp, dtype=jnp.int32)
    tok_slots_t = jnp.sum(jnp.where(tok_exp_t[:, None, :] == experts[None, :, None], slot[None], 0), axis=1)
    partial = jnp.clip(b_start + nb_used - 1, 0, nb_max - 1)
    spare = jnp.minimum(n_used + experts, nb_max - 1)
    fill_blocks = jnp.concatenate([partial, spare]).astype(jnp.int32)
    return (tok_slots_t.T.astype(jnp.int32), fill_blocks, b_start.astype(jnp.int32), nb.astype(jnp.int32),
            n_used.astype(jnp.int32))


def _step_tables(b_start, nb, nb_max, n_chunks, n_used):
    steps = jnp.arange(nb_max * n_chunks, dtype=jnp.int32)
    s_end = jnp.cumsum(nb) * n_chunks
    e = jnp.minimum(jnp.sum((steps[:, None] >= s_end[None, :]).astype(jnp.int32), axis=1), nb.shape[0] - 1)
    local = steps - b_start[e] * n_chunks
    nbe = jnp.maximum(nb[e], 1)
    chunk = (local // nbe).astype(jnp.int32)
    within = local % nbe
    rb = (b_start[e] + within).astype(jnp.int32)
    flags = (within == 0).astype(jnp.int32) + 2 * (rb < n_used).astype(jnp.int32)
    return rb, chunk, e, flags


def _pack_bf16_pairs(x):
    half = x.shape[1] // 2
    bits = lax.bitcast_convert_type(x.astype(BF16).astype(F32), jnp.uint32)
    return (bits[:, :half] >> 16) | bits[:, half:]


def _unpack_bf16_pairs(w):
    lo = lax.bitcast_convert_type(w << 16, F32).astype(BF16)
    hi = lax.bitcast_convert_type(w & jnp.uint32(0xFFFF0000), F32).astype(BF16)
    return lo, hi


def _dispatch_rows_body(fill_ref, idx_ref, h1_ref, hp_ref, xs_hbm, pk, zero, sem, zsem):
    i = pl.program_id(0)
    n = pl.num_programs(0)
    n_rows = idx_ref.shape[2]
    buf = i % 2

    def row_copy(b, t, k):
        return pltpu.make_async_copy(pk.at[b, pl.ds(t, 1)], xs_hbm.at[pl.ds(idx_ref[0, 0, t * TOP_K + k], 1)],
                                     sem.at[b])

    def drain(b):
        def body(t, carry):
            for k in range(TOP_K):
                row_copy(b, t, k).wait()
            return carry
        lax.fori_loop(0, n_rows // TOP_K, body, 0)

    @pl.when(i == 0)
    def _fill():
        zero[...] = jnp.zeros(zero.shape, zero.dtype)

        def body(j, carry):
            blk = pl.ds(pl.multiple_of(fill_ref[j] * MOE_TM, MOE_TM), MOE_TM)
            cp = pltpu.make_async_copy(zero, xs_hbm.at[blk], zsem.at[0])
            cp.start()
            cp.wait()
            return carry
        lax.fori_loop(0, fill_ref.shape[0], body, 0)

    @pl.when(i >= 2)
    def _reuse():
        drain(buf)

    packed = _pack_bf16_pairs(h1_ref[...])
    pk[buf] = packed
    hp_ref[...] = packed

    def issue(t, carry):
        for k in range(TOP_K):
            row_copy(buf, t, k).start()
        return carry
    lax.fori_loop(0, n_rows // TOP_K, issue, 0)

    @pl.when(i == n - 1)
    def _tail():
        drain(1 - buf)
        drain(buf)


def _dispatch_rows(h1, tok_slots, fill_blocks, n_slots):
    n, d = h1.shape
    tt = SCATTER_TT
    nt = n // tt
    assert nt >= 2
    grid_spec = pltpu.PrefetchScalarGridSpec(
        num_scalar_prefetch=1, grid=(nt,),
        in_specs=[pl.BlockSpec((1, 1, tt * TOP_K), lambda i, f: (i, 0, 0), memory_space=pltpu.SMEM),
                  pl.BlockSpec((tt, d), lambda i, f: (i, 0))],
        out_specs=[pl.BlockSpec((tt, d // 2), lambda i, f: (i, 0)), pl.BlockSpec(memory_space=pl.ANY)],
        scratch_shapes=[pltpu.VMEM((2, tt, d // 2), jnp.uint32), pltpu.VMEM((MOE_TM, d // 2), jnp.uint32),
                        pltpu.SemaphoreType.DMA((2,)), pltpu.SemaphoreType.DMA((1,))])
    return pl.pallas_call(
        _dispatch_rows_body, grid_spec=grid_spec,
        out_shape=[jax.ShapeDtypeStruct((n, d // 2), jnp.uint32),
                   jax.ShapeDtypeStruct((n_slots, d // 2), jnp.uint32)],
        compiler_params=_cparams("arbitrary"), name="moe_dispatch")(
            fill_blocks, tok_slots.reshape(nt, 1, tt * TOP_K), h1)


def _moe_up_body(rb_ref, ch_ref, e_ref, flags_ref, x_ref, wg_ref, wu_ref, hb_ref, wgb_ref, wub_ref):
    flags = flags_ref[pl.program_id(0)]

    @pl.when(flags % 2 == 1)
    def _cast():
        wgb_ref[...] = wg_ref[0].astype(BF16)
        wub_ref[...] = wu_ref[0].astype(BF16)

    @pl.when(flags >= 2)
    def _compute():
        lo, hi = _unpack_bf16_pairs(x_ref[...])
        half = lo.shape[1]
        g = _dot(lo, wgb_ref[:half, :]) + _dot(hi, wgb_ref[half:, :])
        u = _dot(lo, wub_ref[:half, :]) + _dot(hi, wub_ref[half:, :])
        hb_ref[...] = (g * _sigmoid(g) * u).astype(hb_ref.dtype)

    @pl.when(flags < 2)
    def _blank():
        hb_ref[...] = jnp.zeros(hb_ref.shape, hb_ref.dtype)


def _moe_up(xp, wg, wu, tables):
    rb, chunk, e, first = tables
    d = wg.shape[1]
    ff = wg.shape[2]
    wspec = pl.BlockSpec((1, d, MOE_TF), lambda s, rb, ch, e, f: (e[s], 0, ch[s]))
    grid_spec = pltpu.PrefetchScalarGridSpec(
        num_scalar_prefetch=4, grid=(rb.shape[0],),
        in_specs=[pl.BlockSpec((MOE_TM, d // 2), lambda s, rb, ch, e, f: (rb[s], 0)), wspec, wspec],
        out_specs=pl.BlockSpec((MOE_TM, MOE_TF), lambda s, rb, ch, e, f: (rb[s], ch[s])),
        scratch_shapes=[pltpu.VMEM((d, MOE_TF), BF16), pltpu.VMEM((d, MOE_TF), BF16)])
    return pl.pallas_call(
        _moe_up_body, grid_spec=grid_spec, out_shape=jax.ShapeDtypeStruct((xp.shape[0], ff), BF16),
        compiler_params=_cparams("arbitrary"), name="moe_up")(rb, chunk, e, first, xp, wg, wu)


def _moe_down_body(rb_ref, e_ref, flags_ref, hb_ref, wd_ref, y_ref, wdb_ref):
    flags = flags_ref[pl.program_id(0)]

    @pl.when(flags % 2 == 1)
    def _cast():
        wdb_ref[...] = wd_ref[0].astype(BF16)

    @pl.when(flags >= 2)
    def _compute():
        y_ref[...] = _dot(hb_ref[...], wdb_ref[...])

    @pl.when(flags < 2)
    def _blank():
        y_ref[...] = jnp.zeros(y_ref.shape, y_ref.dtype)


def _moe_down(hb, wd, tables):
    rb, _, e, first = tables
    ff = hb.shape[1]
    d = wd.shape[2]
    grid_spec = pltpu.PrefetchScalarGridSpec(
        num_scalar_prefetch=3, grid=(rb.shape[0],),
        in_specs=[pl.BlockSpec((MOE_TM, ff), lambda s, rb, e, f: (rb[s], 0)),
                  pl.BlockSpec((1, ff, d), lambda s, rb, e, f: (e[s], 0, 0))],
        out_specs=pl.BlockSpec((MOE_TM, d), lambda s, rb, e, f: (rb[s], 0)),
        scratch_shapes=[pltpu.VMEM((ff, d), BF16)])
    return pl.pallas_call(
        _moe_down_body, grid_spec=grid_spec, out_shape=jax.ShapeDtypeStruct((hb.shape[0], d), F32),
        compiler_params=_cparams("arbitrary"), name="moe_down")(rb, e, first, hb, wd)


def _swiglu_grouped(xp, wg, wu, wd, b_start, nb, nb_max, n_used):
    hb = _moe_up(xp, wg, wu, _step_tables(b_start, nb, nb_max, wg.shape[2] // MOE_TF, n_used))
    return _moe_down(hb, wd, _step_tables(b_start, nb, nb_max, 1, n_used))


def _combine_body(cur_ref, nxt_ref, y_hbm, gate_ref, h1_ref, ysh_ref, g_ref, b_ref, o_ref, gbuf, sem):
    i = pl.program_id(0)
    n = pl.num_programs(0)
    n_rows = cur_ref.shape[2]

    def fetch(idx_ref, buf, t, k):
        return pltpu.make_async_copy(y_hbm.at[pl.ds(idx_ref[0, 0, t * TOP_K + k], 1)],
                                     gbuf.at[buf, k, pl.ds(t, 1)], sem.at[buf])

    def issue(idx_ref, buf):
        def body(t, carry):
            for k in range(TOP_K):
                fetch(idx_ref, buf, t, k).start()
            return carry
        lax.fori_loop(0, n_rows // TOP_K, body, 0)

    @pl.when(i == 0)
    def _first():
        issue(cur_ref, 0)

    @pl.when(i + 1 < n)
    def _ahead():
        issue(nxt_ref, (i + 1) % 2)

    buf = i % 2

    def drain(t, carry):
        for k in range(TOP_K):
            fetch(cur_ref, buf, t, k).wait()
        return carry

    lax.fori_loop(0, n_rows // TOP_K, drain, 0)
    gate = gate_ref[...]
    routed = gbuf[buf, 0] * gate[:, 0:1]
    for k in range(1, TOP_K):
        routed = routed + gbuf[buf, k] * gate[:, k:k + 1]
    o_ref[...] = _layer_norm(ALPHA * h1_ref[...] + (routed + ysh_ref[...]), g_ref[...], b_ref[...])


def _combine_ln2(y, tok_slots, tok_gates, h1, ysh, g, b):
    n, d = h1.shape
    tt = COMBINE_TT
    nt = n // tt
    idx = tok_slots.reshape(nt, 1, tt * TOP_K)
    row = pl.BlockSpec((tt, d), lambda i: (i, 0))
    vec = pl.BlockSpec((1, d), lambda i: (0, 0))
    return pl.pallas_call(
        _combine_body, grid=(nt,),
        in_specs=[pl.BlockSpec((1, 1, tt * TOP_K), lambda i: (i, 0, 0), memory_space=pltpu.SMEM),
                  pl.BlockSpec((1, 1, tt * TOP_K), lambda i: (jnp.minimum(i + 1, nt - 1), 0, 0),
                               memory_space=pltpu.SMEM),
                  pl.BlockSpec(memory_space=pl.ANY), pl.BlockSpec((tt, TOP_K), lambda i: (i, 0)),
                  row, row, vec, vec],
        out_specs=row, out_shape=jax.ShapeDtypeStruct((n, d), F32),
        scratch_shapes=[pltpu.VMEM((2, TOP_K, tt, d), F32), pltpu.SemaphoreType.DMA((2,))],
        compiler_params=_cparams("arbitrary"), name="moe_combine_ln2")(
            idx, idx, y, tok_gates, h1, ysh, g.reshape(1, d), b.reshape(1, d))


def _forward(x_prompt, x_sample, cache_moba_k, cache_moba_v, cache_diff_k, cache_diff_v, page_table,
             ln_in_g, ln_in_b, w_in, diff_lq1, diff_lk1, diff_lq2, diff_lk2, diff_norm_g, w_out,
             ln1_g, ln1_b, w_router, b_router, w_exp_gate, w_exp_up, w_exp_down,
             w_sh_gate, w_sh_up, w_sh_down, ln2_g, ln2_b):
    bsz, s_len, d = x_prompt.shape
    dbsz, t_len, _ = x_sample.shape
    n_p = bsz * s_len
    n_s = dbsz * t_len
    n = n_p + n_s
    mw = d // 2
    m_heads = mw // HEAD_DIM
    d_heads = mw // (2 * HEAD_DIM)
    past = page_table.shape[1] * cache_moba_k.shape[2]
    lyr = 0

    x = jnp.concatenate([x_prompt.reshape(n_p, d), x_sample.reshape(n_s, d)], axis=0)
    pos = jnp.concatenate([jnp.tile(jnp.arange(s_len, dtype=jnp.int32), bsz),
                           jnp.tile(past + jnp.arange(t_len, dtype=jnp.int32), dbsz)])
    tabs = _rope_tables(pos)
    h = _ln_in(x, ln_in_g, ln_in_b)
    w_in_b = w_in[lyr].astype(BF16)
    qscale = 1.0 / math.sqrt(HEAD_DIM)
    nat, trn = False, True
    qm, qm_t = _proj(h, w_in_b, 0, mw, ((BF16, nat), (BF16, trn)), tabs, qscale, "proj_qm")
    km, km_b = _proj(h, w_in_b, mw, mw, ((F32, nat), (BF16, nat)), tabs, 1.0, "proj_km")
    vm, vm_t = _proj(h, w_in_b, 2 * mw, mw, ((F32, nat), (BF16, trn)), None, 1.0, "proj_vm")
    qd, qd_t = _proj(h, w_in_b, 3 * mw, mw, ((BF16, nat), (BF16, trn)), tabs, qscale, "proj_qd")
    kd, kd_b = _proj(h, w_in_b, 4 * mw, mw, ((F32, nat), (BF16, nat)), tabs, 1.0, "proj_kd")
    vd, vd_t = _proj(h, w_in_b, 5 * mw, mw, ((F32, nat), (BF16, trn)), None, 1.0, "proj_vd")

    lams = [v[lyr].reshape(1, HEAD_DIM) for v in (diff_lq1, diff_lk1, diff_lq2, diff_lk2)]
    g_diff = diff_norm_g[lyr].reshape(1, 2 * HEAD_DIM)

    nbp = s_len // MOBA_BLOCK
    means = _moba_means(km, n_p).reshape(bsz, nbp, mw)
    means = jnp.pad(means, ((0, 0), (0, (-nbp) % 8), (0, 0)))
    om_p = _moba_prompt(qm_t, km_b, vm_t, means, bsz, s_len)
    od_p = _diff_prompt(qd_t, kd_b, vd_t, g_diff, lams, bsz, s_len)

    def smp(a, w):
        return a[n_p:].reshape(dbsz, t_len * (mw // w), w)

    hw = 2 * HEAD_DIM
    om_s, od_s = _sample_attn(page_table, smp(qm, HEAD_DIM), smp(qd, hw), smp(km, HEAD_DIM), smp(vm, HEAD_DIM),
                              smp(kd, hw), smp(vd, hw), cache_moba_k, cache_moba_v, cache_diff_k,
                              cache_diff_v, g_diff, lams, lyr, t_len)
    o_s = jnp.concatenate([om_s.reshape(n_s, mw), od_s.reshape(n_s, mw)], axis=1)

    o_all = jnp.concatenate([jnp.concatenate([om_p, od_p], axis=1), o_s.astype(BF16)], axis=0)
    (a,) = _proj(o_all, w_out[lyr].astype(BF16), 0, d, ((F32, nat),), None, 1.0, "proj_out")
    h1, tok_exp_t, tok_gate_t, sel_t = _ln1_router(x, a, ln_in_g, ln_in_b, ln1_g[lyr], ln1_b[lyr],
                                                   w_router[lyr], b_router[lyr])

    tok_slots, fill_blocks, b_start, nb, n_used = _dispatch_tables(sel_t, tok_exp_t)
    nb_max = n * TOP_K // MOE_TM + sel_t.shape[0]
    h1p, xs = _dispatch_rows(h1, tok_slots, fill_blocks, nb_max * MOE_TM)
    y = _swiglu_grouped(xs, w_exp_gate[lyr], w_exp_up[lyr], w_exp_down[lyr], b_start, nb, nb_max, n_used)
    nb_sh = n // MOE_TM
    ysh = _swiglu_grouped(h1p, w_sh_gate[lyr][None], w_sh_up[lyr][None], w_sh_down[lyr][None],
                          jnp.zeros((1,), jnp.int32), jnp.full((1,), nb_sh, jnp.int32), nb_sh, nb_sh)
    tok_gates = tok_gate_t.T
    out = _combine_ln2(y, tok_slots, tok_gates, h1, ysh, ln2_g[lyr], ln2_b[lyr])

    def prm(t, nh):
        return t[:n_p].reshape(1, bsz, s_len, nh, mw // nh)

    def dec(t, nh):
        return t[n_p:].reshape(1, dbsz, t_len, nh, mw // nh)

    outs = (out[:n_p].reshape(bsz, s_len, d), out[n_p:].reshape(dbsz, t_len, d),
            prm(km, m_heads), prm(vm, m_heads), prm(kd, d_heads), prm(vd, d_heads),
            dec(km, m_heads), dec(vm, m_heads), dec(kd, d_heads), dec(vd, d_heads))
    mid = dict(a=a, h1=h1, y=y, tok_slots=tok_slots, tok_gates=tok_gates, ysh=ysh)
    return outs, mid


def kernel(x_prompt, x_sample, cache_moba_k, cache_moba_v, cache_diff_k, cache_diff_v, page_table, ln_in_g, ln_in_b, w_in, diff_lq1, diff_lk1, diff_lq2, diff_lk2, diff_norm_g, w_out, ln1_g, ln1_b, w_router, b_router, w_exp_gate, w_exp_up, w_exp_down, w_sh_gate, w_sh_up, w_sh_down, ln2_g, ln2_b):
    outs, _ = _forward(x_prompt, x_sample, cache_moba_k, cache_moba_v, cache_diff_k, cache_diff_v, page_table,
                       ln_in_g, ln_in_b, w_in, diff_lq1, diff_lk1, diff_lq2, diff_lk2, diff_norm_g, w_out,
                       ln1_g, ln1_b, w_router, b_router, w_exp_gate, w_exp_up, w_exp_down,
                       w_sh_gate, w_sh_up, w_sh_down, ln2_g, ln2_b)
    return outs
```

```python
import functools
import math

import jax
import jax.numpy as jnp
from jax import lax
from jax.experimental import pallas as pl
from jax.experimental.pallas import tpu as pltpu

F32 = jnp.float32
BF16 = jnp.bfloat16

HEAD_DIM = 128
MOBA_BLOCK = 256
MOBA_TOPK = 3
ROPE_THETA = 500000.0
ROPE_DIV = 4
N_GROUPS = 8
TOPK_GROUPS = 4
TOP_K = 8
ROUTED_SCALE = 2.5
LN_EPS = 1e-5
NEG = -1e30
DEPTH = 1
ALPHA = (2.0 * DEPTH) ** 0.25
LAM_INIT = 0.8 - 0.6 * math.exp(-0.3 * 0)

VMEM_LIMIT_BYTES = 56 * 1024 * 1024
LANES = 128

ROW_TILE = 256
MM_TM = 512
MM_TN = 1024
ATT_TQ = MOBA_BLOCK
MOBA_HEADS_PER_STEP = 4
DIFF_HEADS_PER_STEP = 2
MOE_TM = 256
MOE_TF = 512
SCATTER_TT = 64
COMBINE_TT = 32


def _cparams(*sem):
    return pltpu.CompilerParams(dimension_semantics=sem, vmem_limit_bytes=VMEM_LIMIT_BYTES)


def _dot(a, b):
    return jnp.dot(a, b, preferred_element_type=F32)


def _dot_nt(a, b):
    return lax.dot_general(a, b, (((1,), (1,)), ((), ())), preferred_element_type=F32)


def _layer_norm(x, g, b):
    mu = jnp.mean(x, axis=-1, keepdims=True)
    xc = x - mu
    var = jnp.mean(xc * xc, axis=-1, keepdims=True)
    return xc * lax.rsqrt(var + LN_EPS) * g + b


def _sigmoid(x):
    return 1.0 / (1.0 + jnp.exp(-x))


def _two_stream_specs(n_p, n_s, d):
    assert n_p % ROW_TILE == 0 and n_s % ROW_TILE == 0
    npb = n_p // ROW_TILE
    return (npb, pl.BlockSpec((ROW_TILE, d), lambda i: (jnp.minimum(i, npb - 1), 0)),
            pl.BlockSpec((ROW_TILE, d), lambda i: (jnp.maximum(i - npb, 0), 0)))


def _ln_in_body(xp_ref, xs_ref, g_ref, b_ref, o_ref, *, npb):
    x = jnp.where(pl.program_id(0) < npb, xp_ref[...], xs_ref[...])
    o_ref[...] = _layer_norm(x, g_ref[...], b_ref[...]).astype(o_ref.dtype)


def _ln_in(xp, xs, g, b):
    (n_p, d), n_s = xp.shape, xs.shape[0]
    npb, p_spec, s_spec = _two_stream_specs(n_p, n_s, d)
    vec = pl.BlockSpec((1, d), lambda i: (0, 0))
    return pl.pallas_call(
        functools.partial(_ln_in_body, npb=npb), grid=((n_p + n_s) // ROW_TILE,),
        in_specs=[p_spec, s_spec, vec, vec], out_specs=pl.BlockSpec((ROW_TILE, d), lambda i: (i, 0)),
        out_shape=jax.ShapeDtypeStruct((n_p + n_s, d), BF16), compiler_params=_cparams("parallel"),
        name="ln_in")(xp, xs, g.reshape(1, d), b.reshape(1, d))


def _rope_chunk(y, c, sa, sb):
    return y * c + pltpu.roll(y, HEAD_DIM - 16, 1) * sa + pltpu.roll(y, 16, 1) * sb


def _proj_body(*refs, rope, scale, transposed):
    if rope:
        h_ref, w_ref, c_ref, sa_ref, sb_ref = refs[:5]
    else:
        h_ref, w_ref = refs[:2]
    out_refs = refs[len(refs) - len(transposed):]
    acc = _dot(h_ref[...], w_ref[...])
    if rope:
        c, sa, sb = c_ref[...], sa_ref[...], sb_ref[...]
        acc = jnp.concatenate(
            [_rope_chunk(acc[:, k * HEAD_DIM:(k + 1) * HEAD_DIM], c, sa, sb)
             for k in range(acc.shape[1] // HEAD_DIM)], axis=1)
    if scale != 1.0:
        acc = acc * scale
    acc_t = acc.T if any(transposed) else None
    for o_ref, tr in zip(out_refs, transposed):
        o_ref[...] = (acc_t if tr else acc).astype(o_ref.dtype)


def _proj(h, w, col0, ncols, outs, rope_tabs=None, scale=1.0, name="proj"):
    n, k = h.shape
    tm, tn = MM_TM, MM_TN
    cb = col0 // tn
    in_specs = [pl.BlockSpec((tm, k), lambda i, j: (i, 0)),
                pl.BlockSpec((k, tn), lambda i, j: (0, cb + j))]
    args = [h, w]
    if rope_tabs is not None:
        in_specs += [pl.BlockSpec((tm, HEAD_DIM), lambda i, j: (i, 0))] * 3
        args += list(rope_tabs)
    out_specs = [pl.BlockSpec((tn, tm), lambda i, j: (j, i)) if tr else pl.BlockSpec((tm, tn), lambda i, j: (i, j))
                 for _, tr in outs]
    out_shape = [jax.ShapeDtypeStruct((ncols, n) if tr else (n, ncols), dt) for dt, tr in outs]
    return pl.pallas_call(
        functools.partial(_proj_body, rope=rope_tabs is not None, scale=scale,
                          transposed=tuple(tr for _, tr in outs)),
        grid=(n // tm, ncols // tn), in_specs=in_specs, out_specs=out_specs, out_shape=out_shape,
        compiler_params=_cparams("parallel", "arbitrary"), name=name)(*args)


def _rope_tables(pos):
    rot = HEAD_DIM // ROPE_DIV
    half = rot // 2
    inv_freq = ROPE_THETA ** (-jnp.arange(half, dtype=F32) * (2.0 / rot))
    ang = pos.astype(F32)[:, None] * inv_freq[None, :]
    cos, sin = jnp.cos(ang), jnp.sin(ang)
    n = pos.shape[0]
    ones = jnp.ones((n, HEAD_DIM - rot), F32)
    zeros = jnp.zeros((n, HEAD_DIM - rot), F32)
    zh = jnp.zeros((n, half), F32)
    c = jnp.concatenate([cos, cos, ones], axis=1)
    sa = jnp.concatenate([-sin, zh, zeros], axis=1)
    sb = jnp.concatenate([zh, sin, zeros], axis=1)
    return c, sa, sb


def _means_body(k_ref, o_ref):
    o_ref[0] = jnp.mean(k_ref[...], axis=0, keepdims=True)


def _moba_means(k, n_rows):
    w = k.shape[1]
    nb = n_rows // MOBA_BLOCK
    return pl.pallas_call(
        _means_body, grid=(nb,), in_specs=[pl.BlockSpec((MOBA_BLOCK, w), lambda i: (i, 0))],
        out_specs=pl.BlockSpec((1, 1, w), lambda i: (i, 0, 0)),
        out_shape=jax.ShapeDtypeStruct((nb, 1, w), F32), compiler_params=_cparams("parallel"),
        name="moba_means")(k)


def _pick_top(cur, idx_f, n_pick, axis, big):
    sel = jnp.zeros(cur.shape, jnp.bool_)
    for _ in range(n_pick):
        mx = jnp.max(cur, axis=axis, keepdims=True)
        first = jnp.min(jnp.where(cur == mx, idx_f, big), axis=axis, keepdims=True)
        pick = idx_f == first
        sel = jnp.logical_or(sel, pick)
        cur = jnp.where(pick, -jnp.inf, cur)
    return sel


def _row_reduce(x, op, reduce):
    if x.shape[1] % LANES:
        return reduce(x, axis=1, keepdims=True)
    folded = x[:, :LANES]
    for j in range(1, x.shape[1] // LANES):
        folded = op(folded, x[:, j * LANES:(j + 1) * LANES])
    return reduce(folded, axis=1, keepdims=True)


def _softmax_step_t(st, vt, m_ref, l_ref, acc_ref, c, first):
    if first:
        m_new = jnp.max(st, axis=0, keepdims=True)
        p = jnp.exp(st - m_new)
        l_ref[c] = jnp.sum(p, axis=0, keepdims=True)
        acc_ref[c] = _dot(vt, p.astype(BF16))
    else:
        m_old = m_ref[c]
        m_new = jnp.maximum(m_old, jnp.max(st, axis=0, keepdims=True))
        alpha = jnp.exp(m_old - m_new)
        p = jnp.exp(st - m_new)
        l_ref[c] = alpha * l_ref[c] + jnp.sum(p, axis=0, keepdims=True)
        acc_ref[c] = alpha * acc_ref[c] + _dot(vt, p.astype(BF16))
    m_ref[c] = m_new


def _moba_prompt_body(qt_ref, k_ref, vt_ref, mean_ref, o_ref, sel_ref, m_ref, l_ref, acc_ref):
    qi = pl.program_id(2)
    tq = qt_ref.shape[1]
    n_h = qt_ref.shape[0] // HEAD_DIM
    n_blk = mean_ref.shape[1]
    blk_id = lax.broadcasted_iota(jnp.int32, (n_blk, tq), 0)
    past = blk_id < qi
    key = lax.broadcasted_iota(jnp.int32, (tq, tq), 0)
    qry = lax.broadcasted_iota(jnp.int32, (tq, tq), 1)
    own = pl.ds(pl.multiple_of(qi * tq, tq), tq)
    hs = [slice(h * HEAD_DIM, (h + 1) * HEAD_DIM) for h in range(n_h)]
    for h in range(n_h):
        qt = qt_ref[hs[h], :]
        mean = mean_ref[0, :, hs[h]]
        m_hi = mean.astype(BF16)
        m_lo = (mean - m_hi.astype(F32)).astype(BF16)
        gate = _dot(m_hi, qt) + _dot(m_lo, qt)
        sel = _pick_top(jnp.where(past, gate, -jnp.inf), blk_id.astype(F32), MOBA_TOPK, 0, float(n_blk))
        sel_ref[h] = jnp.where(jnp.logical_and(sel, past), 1.0, 0.0)
        st = jnp.where(key <= qry, _dot(k_ref[own, hs[h]], qt), NEG)
        _softmax_step_t(st, vt_ref[hs[h], own], m_ref, l_ref, acc_ref, h, True)

    def body(ki, carry):
        blk = pl.ds(pl.multiple_of(ki * tq, tq), tq)
        picked = [sel_ref[h, pl.ds(ki, 1), :] > 0.5 for h in range(n_h)]
        for h in range(n_h):
            st = jnp.where(picked[h], _dot(k_ref[blk, hs[h]], qt_ref[hs[h], :]), NEG)
            _softmax_step_t(st, vt_ref[hs[h], blk], m_ref, l_ref, acc_ref, h, False)
        return carry

    lax.fori_loop(0, qi, body, 0)
    for h in range(n_h):
        o_ref[:, hs[h]] = (acc_ref[h] / l_ref[h]).T.astype(o_ref.dtype)


def _moba_prompt(qt, k, vt, means, bsz, s_len):
    hb = MOBA_HEADS_PER_STEP
    w = hb * HEAD_DIM
    groups = qt.shape[0] // w
    nq = s_len // ATT_TQ
    return pl.pallas_call(
        _moba_prompt_body, grid=(bsz, groups, nq),
        in_specs=[pl.BlockSpec((w, ATT_TQ), lambda b, h, i: (h, b * nq + i)),
                  pl.BlockSpec((s_len, w), lambda b, h, i: (b, h)),
                  pl.BlockSpec((w, s_len), lambda b, h, i: (h, b)),
                  pl.BlockSpec((1, means.shape[1], w), lambda b, h, i: (b, 0, h))],
        out_specs=pl.BlockSpec((ATT_TQ, w), lambda b, h, i: (b * nq + i, h)),
        out_shape=jax.ShapeDtypeStruct((bsz * s_len, qt.shape[0]), BF16),
        scratch_shapes=[pltpu.VMEM((hb, means.shape[1], ATT_TQ), F32), pltpu.VMEM((hb, 1, ATT_TQ), F32),
                        pltpu.VMEM((hb, 1, ATT_TQ), F32), pltpu.VMEM((hb, HEAD_DIM, ATT_TQ), F32)],
        compiler_params=_cparams("parallel", "parallel", "arbitrary"), name="moba_prompt")(qt, k, vt, means)


def _lambda(lq1, lk1, lq2, lk2):
    return (jnp.exp(jnp.sum(lq1 * lk1, axis=1, keepdims=True))
            - jnp.exp(jnp.sum(lq2 * lk2, axis=1, keepdims=True)) + LAM_INIT)


def _diff_out(acc0, l0, acc1, l1, lam, g):
    o = acc0 / l0 - lam * (acc1 / l1)
    ms = jnp.mean(o * o, axis=1, keepdims=True)
    return o * lax.rsqrt(ms + LN_EPS) * g * (1.0 - LAM_INIT)


def _diff_prompt_body(qt_ref, k_ref, vt_ref, gt_ref, lq1, lk1, lq2, lk2, o_ref, m_ref, l_ref, acc_ref):
    qi = pl.program_id(2)
    tq = qt_ref.shape[1]
    hw = 2 * HEAD_DIM
    n_h = qt_ref.shape[0] // hw
    key = lax.broadcasted_iota(jnp.int32, (tq, tq), 0)
    qry = lax.broadcasted_iota(jnp.int32, (tq, tq), 1)
    own = pl.ds(pl.multiple_of(qi * tq, tq), tq)

    def comp(h, c):
        return slice(h * hw + c * HEAD_DIM, h * hw + (c + 1) * HEAD_DIM)

    def head(h):
        return slice(h * hw, (h + 1) * hw)

    for h in range(n_h):
        for c in range(2):
            st = jnp.where(key <= qry, _dot(k_ref[own, comp(h, c)], qt_ref[comp(h, c), :]), NEG)
            _softmax_step_t(st, vt_ref[head(h), own], m_ref, l_ref, acc_ref, 2 * h + c, True)

    def body(ki, carry):
        blk = pl.ds(pl.multiple_of(ki * tq, tq), tq)
        for h in range(n_h):
            for c in range(2):
                st = _dot(k_ref[blk, comp(h, c)], qt_ref[comp(h, c), :])
                _softmax_step_t(st, vt_ref[head(h), blk], m_ref, l_ref, acc_ref, 2 * h + c, False)
        return carry

    lax.fori_loop(0, qi, body, 0)
    lam = _lambda(lq1[...], lk1[...], lq2[...], lk2[...])
    for h in range(n_h):
        ot = acc_ref[2 * h] / l_ref[2 * h] - lam * (acc_ref[2 * h + 1] / l_ref[2 * h + 1])
        ms = jnp.mean(ot * ot, axis=0, keepdims=True)
        ot = ot * lax.rsqrt(ms + LN_EPS) * gt_ref[...] * (1.0 - LAM_INIT)
        o_ref[:, head(h)] = ot.T.astype(o_ref.dtype)


def _diff_prompt(qt, k, vt, g, lams, bsz, s_len):
    hw = 2 * HEAD_DIM
    hb = DIFF_HEADS_PER_STEP
    w = hb * hw
    groups = qt.shape[0] // w
    nq = s_len // ATT_TQ
    vec = pl.BlockSpec((1, HEAD_DIM), lambda b, h, i: (0, 0))
    return pl.pallas_call(
        _diff_prompt_body, grid=(bsz, groups, nq),
        in_specs=[pl.BlockSpec((w, ATT_TQ), lambda b, h, i: (h, b * nq + i)),
                  pl.BlockSpec((s_len, w), lambda b, h, i: (b, h)),
                  pl.BlockSpec((w, s_len), lambda b, h, i: (h, b)),
                  pl.BlockSpec((hw, 1), lambda b, h, i: (0, 0)), vec, vec, vec, vec],
        out_specs=pl.BlockSpec((ATT_TQ, w), lambda b, h, i: (b * nq + i, h)),
        out_shape=jax.ShapeDtypeStruct((bsz * s_len, qt.shape[0]), BF16),
        scratch_shapes=[pltpu.VMEM((2 * hb, 1, ATT_TQ), F32), pltpu.VMEM((2 * hb, 1, ATT_TQ), F32),
                        pltpu.VMEM((2 * hb, hw, ATT_TQ), F32)],
        compiler_params=_cparams("parallel", "parallel", "arbitrary"), name="diff_prompt")(
            qt, k, vt, g.reshape(hw, 1), *lams)


def _sample_attn_body(pt_ref, qm_ref, qd_ref, kmn_ref, vmn_ref, kdn_ref, vdn_ref,
                      mk0, mk1, mv0, mv1, dk0, dk1, dv0, dv1, g_ref, lq1, lk1, lq2, lk2, hm_ref, hd_ref,
                      om_ref, od_ref, gate_ref, mall_ref, lall_ref, oblk_ref, md_ref, ld_ref, accd_ref,
                      *, t_len):
    del pt_ref
    n = pl.program_id(1)
    n_blk = pl.num_programs(1)
    page, m_heads, _ = mk0.shape[2:]
    d_heads, hw = dk0.shape[3:]
    qm = qm_ref[0]
    qd = qd_ref[0]
    rm, rd = qm.shape[0], qd.shape[0]
    lane = lax.broadcasted_iota(jnp.int32, (rm, LANES), 1)

    def head_match(rows, cols, heads):
        r = lax.broadcasted_iota(jnp.int32, (rows, cols), 0)
        c = lax.broadcasted_iota(jnp.int32, (rows, cols), 1)
        return r, c, (r % heads) == (c % heads)

    def rows2d(ref):
        x = ref[0, 0]
        return x.reshape(x.shape[0] * x.shape[1], x.shape[2])

    @pl.when(n == 0)
    def _init():
        gate_ref[...] = jnp.zeros(gate_ref.shape, F32)
        mall_ref[...] = jnp.zeros(mall_ref.shape, F32)
        lall_ref[...] = jnp.zeros(lall_ref.shape, F32)
        md_ref[...] = jnp.full(md_ref.shape, NEG, F32)
        ld_ref[...] = jnp.zeros(ld_ref.shape, F32)
        accd_ref[...] = jnp.zeros(accd_ref.shape, F32)

    k0, k1 = mk0[0, 0], mk1[0, 0]
    mean = (jnp.sum(k0, axis=0) + jnp.sum(k1, axis=0)) * (1.0 / (2 * page))
    mean_rows = jnp.concatenate([mean] * t_len, axis=0)
    gate_n = jnp.sum(qm.astype(F32) * mean_rows, axis=1, keepdims=True)
    kb = jnp.concatenate([rows2d(mk0), rows2d(mk1)], axis=0).astype(BF16)
    vb = jnp.concatenate([rows2d(mv0), rows2d(mv1)], axis=0).astype(BF16)
    s = jnp.where(hm_ref[...] > 0.5, _dot_nt(qm, kb), NEG)
    m_n = _row_reduce(s, jnp.maximum, jnp.max)
    p = jnp.exp(s - m_n)
    l_n = _row_reduce(p, jnp.add, jnp.sum)
    oblk_ref[n] = _dot(p.astype(BF16), vb)
    here = lane == n
    gate_ref[...] = jnp.where(here, gate_n, gate_ref[...])
    mall_ref[...] = jnp.where(here, m_n, mall_ref[...])
    lall_ref[...] = jnp.where(here, l_n, lall_ref[...])

    kd = jnp.concatenate([rows2d(dk0), rows2d(dk1)], axis=0).astype(BF16)
    vd = jnp.concatenate([rows2d(dv0), rows2d(dv1)], axis=0).astype(BF16)
    same_d = hd_ref[...] > 0.5

    def diff_update(c, s, v):
        m_old = md_ref[c]
        m_new = jnp.maximum(m_old, _row_reduce(s, jnp.maximum, jnp.max))
        alpha = jnp.exp(m_old - m_new)
        p = jnp.exp(s - m_new)
        ld_ref[c] = alpha * ld_ref[c] + _row_reduce(p, jnp.add, jnp.sum)
        accd_ref[c] = alpha * accd_ref[c] + _dot(p.astype(BF16), v)
        md_ref[c] = m_new

    for c in range(2):
        cs = slice(c * HEAD_DIM, (c + 1) * HEAD_DIM)
        diff_update(c, jnp.where(same_d, _dot_nt(qd[:, cs], kd[:, cs]), NEG), vd)

    @pl.when(n == n_blk - 1)
    def _finish():
        r, c, same = head_match(rm, rm, m_heads)
        own_ok = jnp.logical_and(same, c // m_heads <= r // m_heads)
        s_own = jnp.where(own_ok, _dot_nt(qm, kmn_ref[0].astype(BF16)), NEG)
        cached = lane < n_blk
        sel = _pick_top(jnp.where(cached, gate_ref[...], -jnp.inf), lane.astype(F32), MOBA_TOPK, 1,
                        float(LANES))
        sel = jnp.logical_and(sel, cached)
        mall = mall_ref[...]
        m_tot = jnp.maximum(jnp.max(jnp.where(sel, mall, NEG), axis=1, keepdims=True),
                            jnp.max(s_own, axis=1, keepdims=True))
        w_blk = jnp.where(sel, jnp.exp(mall - m_tot), 0.0)
        p_own = jnp.exp(s_own - m_tot)
        denom = (jnp.sum(w_blk * lall_ref[...], axis=1, keepdims=True)
                 + jnp.sum(p_own, axis=1, keepdims=True))
        o_m = _dot(p_own.astype(BF16), vmn_ref[0].astype(BF16))
        for b in range(oblk_ref.shape[0]):
            o_m = o_m + w_blk[:, b:b + 1] * oblk_ref[b]
        om_ref[0] = o_m / denom

        r, c, same = head_match(rd, rd, d_heads)
        own_ok = jnp.logical_and(same, c // d_heads <= r // d_heads)
        kdn = kdn_ref[0].astype(BF16)
        vdn = vdn_ref[0].astype(BF16)
        for comp in range(2):
            cs = slice(comp * HEAD_DIM, (comp + 1) * HEAD_DIM)
            diff_update(comp, jnp.where(own_ok, _dot_nt(qd[:, cs], kdn[:, cs]), NEG), vdn)
        lam = _lambda(lq1[...], lk1[...], lq2[...], lk2[...])
        od_ref[0] = _diff_out(accd_ref[0], ld_ref[0], accd_ref[1], ld_ref[1], lam, g_ref[...])


def _sample_attn(page_table, qm, qd, kmn, vmn, kdn, vdn, cmk, cmv, cdk, cdv, g, lams, lyr, t_len):
    dbsz, rm, _ = qm.shape
    rd, hw = qd.shape[1:]
    n_pages = page_table.shape[1]
    page, m_heads = cmk.shape[2:4]
    d_heads = cdk.shape[3]
    ppb = MOBA_BLOCK // page
    assert ppb == 2 and (n_pages * page) % MOBA_BLOCK == 0 and t_len <= MOBA_BLOCK
    n_blk = n_pages // ppb
    assert MOBA_TOPK <= n_blk <= LANES

    def head_mask(rows, heads):
        keys = jnp.arange(ppb * page * heads, dtype=jnp.int32)
        return (jnp.arange(rows, dtype=jnp.int32)[:, None] % heads == keys[None, :] % heads).astype(F32)

    head_m, head_d = head_mask(rm, m_heads), head_mask(rd, d_heads)

    def tok_spec(rows, w):
        return pl.BlockSpec((1, rows, w), lambda s, n, pt: (s, 0, 0))

    def page_spec(j, heads, w):
        return pl.BlockSpec((1, 1, page, heads, w),
                            lambda s, n, pt: (lyr, pt[s * n_pages + ppb * n + j], 0, 0, 0))

    mp = [page_spec(0, m_heads, HEAD_DIM), page_spec(1, m_heads, HEAD_DIM)]
    dp = [page_spec(0, d_heads, hw), page_spec(1, d_heads, hw)]
    vec = pl.BlockSpec((1, HEAD_DIM), lambda s, n, pt: (0, 0))
    grid_spec = pltpu.PrefetchScalarGridSpec(
        num_scalar_prefetch=1, grid=(dbsz, n_blk),
        in_specs=[tok_spec(rm, HEAD_DIM), tok_spec(rd, hw), tok_spec(rm, HEAD_DIM), tok_spec(rm, HEAD_DIM),
                  tok_spec(rd, hw), tok_spec(rd, hw)] + mp + mp + dp + dp
        + [pl.BlockSpec((1, hw), lambda s, n, pt: (0, 0)), vec, vec, vec, vec,
           pl.BlockSpec(head_m.shape, lambda s, n, pt: (0, 0)), pl.BlockSpec(head_d.shape, lambda s, n, pt: (0, 0))],
        out_specs=[tok_spec(rm, HEAD_DIM), tok_spec(rd, hw)],
        scratch_shapes=[pltpu.VMEM((rm, LANES), F32), pltpu.VMEM((rm, LANES), F32),
                        pltpu.VMEM((rm, LANES), F32), pltpu.VMEM((n_blk, rm, HEAD_DIM), F32),
                        pltpu.VMEM((2, rd, 1), F32), pltpu.VMEM((2, rd, 1), F32),
                        pltpu.VMEM((2, rd, hw), F32)])
    return pl.pallas_call(
        functools.partial(_sample_attn_body, t_len=t_len), grid_spec=grid_spec,
        out_shape=[jax.ShapeDtypeStruct((dbsz, rm, HEAD_DIM), F32), jax.ShapeDtypeStruct((dbsz, rd, hw), F32)],
        compiler_params=_cparams("parallel", "arbitrary"), name="sample_attn")(
            page_table.reshape(-1), qm, qd, kmn, vmn, kdn, vdn,
            cmk, cmk, cmv, cmv, cdk, cdk, cdv, cdv, g, *lams, head_m, head_d)


def _ln1_router_body(xp_ref, xs_ref, a_ref, gi_ref, bi_ref, g1_ref, b1_ref, wh_ref, wl_ref, br_ref,
                     h1_ref, exp_ref, gate_ref, sel_ref, *, npb):
    x = jnp.where(pl.program_id(0) < npb, xp_ref[...], xs_ref[...])
    h = _layer_norm(x, gi_ref[...], bi_ref[...])
    z = _layer_norm(ALPHA * h + a_ref[...], g1_ref[...], b1_ref[...])
    h1_ref[...] = z
    z_hi = z.astype(BF16)
    z_lo = (z - z_hi.astype(F32)).astype(BF16)
    w_hi = wh_ref[...]
    logits = _dot_nt(w_hi, z_hi) + _dot_nt(w_hi, z_lo) + _dot_nt(wl_ref[...], z_hi)
    scores = _sigmoid(logits)
    choice = scores + br_ref[...]
    n_exp, tm = choice.shape
    per = n_exp // N_GROUPS
    c3 = choice.reshape(N_GROUPS, per, tm)
    sub = lax.broadcasted_iota(jnp.int32, c3.shape, 1).astype(F32)
    top2 = _pick_top(c3, sub, 2, 1, float(per))
    grp = jnp.sum(jnp.where(top2, c3, 0.0), axis=1)
    gidx = lax.broadcasted_iota(jnp.int32, grp.shape, 0).astype(F32)
    keep_g = _pick_top(grp, gidx, TOPK_GROUPS, 0, float(N_GROUPS))
    keep = jnp.broadcast_to(keep_g[:, None, :], c3.shape).reshape(n_exp, tm)
    eidx = lax.broadcasted_iota(jnp.int32, choice.shape, 0).astype(F32)
    cur = jnp.where(keep, choice, NEG)
    sel = jnp.zeros(cur.shape, jnp.bool_)
    ids, ws = [], []
    for _ in range(TOP_K):
        mx = jnp.max(cur, axis=0, keepdims=True)
        first = jnp.min(jnp.where(cur == mx, eidx, float(n_exp)), axis=0, keepdims=True)
        pick = eidx == first
        sel = jnp.logical_or(sel, pick)
        ids.append(first)
        ws.append(jnp.sum(jnp.where(pick, scores, 0.0), axis=0, keepdims=True))
        cur = jnp.where(pick, -jnp.inf, cur)
    wsel = jnp.concatenate(ws, axis=0)
    exp_ref[...] = jnp.concatenate(ids, axis=0).astype(jnp.int32)
    gate_ref[...] = wsel / jnp.sum(wsel, axis=0, keepdims=True) * ROUTED_SCALE
    sel_ref[...] = jnp.where(sel, 1.0, 0.0)


def _ln1_router(xp, xs, a, gi, bi, g1, b1, w_router, b_router):
    n, d = a.shape
    npb, p_spec, s_spec = _two_stream_specs(xp.shape[0], xs.shape[0], d)
    n_exp = w_router.shape[1]
    wt = w_router.T
    w_hi = wt.astype(BF16)
    w_lo = (wt - w_hi.astype(F32)).astype(BF16)
    row = pl.BlockSpec((ROW_TILE, d), lambda i: (i, 0))
    vec = pl.BlockSpec((1, d), lambda i: (0, 0))
    wspec = pl.BlockSpec((n_exp, d), lambda i: (0, 0))
    ex = pl.BlockSpec((n_exp, ROW_TILE), lambda i: (0, i))
    tk = pl.BlockSpec((TOP_K, ROW_TILE), lambda i: (0, i))
    return pl.pallas_call(
        functools.partial(_ln1_router_body, npb=npb), grid=(n // ROW_TILE,),
        in_specs=[p_spec, s_spec, row, vec, vec, vec, vec, wspec, wspec,
                  pl.BlockSpec((n_exp, 1), lambda i: (0, 0))],
        out_specs=[row, tk, tk, ex],
        out_shape=[jax.ShapeDtypeStruct((n, d), F32), jax.ShapeDtypeStruct((TOP_K, n), jnp.int32),
                   jax.ShapeDtypeStruct((TOP_K, n), F32), jax.ShapeDtypeStruct((n_exp, n), F32)],
        compiler_params=_cparams("parallel"), name="ln1_router")(
            xp, xs, a, gi.reshape(1, d), bi.reshape(1, d), g1.reshape(1, d), b1.reshape(1, d),
            w_hi, w_lo, b_router.reshape(n_exp, 1))


def _dispatch_tables(sel_t, tok_exp_t):
    n_exp, n_tok = sel_t.shape
    nb_max = n_tok * TOP_K // MOE_TM + n_exp
    mi = (sel_t > 0.5).astype(jnp.int32)
    rank = jnp.cumsum(mi, axis=1) - mi
    counts = jnp.sum(mi, axis=1)
    nb_used = (counts + MOE_TM - 1) // MOE_TM
    n_used = jnp.sum(nb_used)
    nb = nb_used.at[n_exp - 1].add(nb_max - n_used)
    b_end = jnp.cumsum(nb)
    b_start = b_end - nb
    slot = b_start[:, None] * MOE_TM + rank
    experts = jnp.arange(n_exp, dtype=jnp.int32)
    tok_slots_t = jnp.sum(jnp.where(tok_exp_t[:, None, :] == experts[None, :, None], slot[None], 0), axis=1)
    partial = jnp.clip(b_start + nb_used - 1, 0, nb_max - 1)
    spare = jnp.minimum(n_used + experts, nb_max - 1)
    fill_blocks = jnp.concatenate([partial, spare]).astype(jnp.int32)
    return (tok_slots_t.T.astype(jnp.int32), fill_blocks, b_start.astype(jnp.int32), nb.astype(jnp.int32),
            n_used.astype(jnp.int32))


def _step_tables(b_start, nb, nb_max, n_chunks, n_used):
    steps = jnp.arange(nb_max * n_chunks, dtype=jnp.int32)
    s_end = jnp.cumsum(nb) * n_chunks
    e = jnp.minimum(jnp.sum((steps[:, None] >= s_end[None, :]).astype(jnp.int32), axis=1), nb.shape[0] - 1)
    local = steps - b_start[e] * n_chunks
    nbe = jnp.maximum(nb[e], 1)
    chunk = (local // nbe).astype(jnp.int32)
    within = local % nbe
    rb = (b_start[e] + within).astype(jnp.int32)
    n_steps = steps.shape[0]
    follow = steps - within + nbe
    has_next = follow < n_steps
    follow = jnp.minimum(follow, n_steps - 1)
    flags = ((within == 0).astype(jnp.int32) + 2 * (rb < n_used).astype(jnp.int32)
             + 4 * has_next.astype(jnp.int32))
    return rb, chunk, e, flags, e[follow], chunk[follow]


def _pack_bf16_pairs(x):
    half = x.shape[1] // 2
    bits = lax.bitcast_convert_type(x.astype(BF16).astype(F32), jnp.uint32)
    return (bits[:, :half] >> 16) | bits[:, half:]


def _unpack_bf16_pairs(w):
    lo = lax.bitcast_convert_type(w << 16, F32).astype(BF16)
    hi = lax.bitcast_convert_type(w & jnp.uint32(0xFFFF0000), F32).astype(BF16)
    return lo, hi


def _dispatch_rows_body(fill_ref, idx_ref, h1_ref, hp_ref, xs_hbm, pk, zero, sem, zsem):
    i = pl.program_id(0)
    n = pl.num_programs(0)
    n_rows = idx_ref.shape[2]
    buf = i % 2

    def row_copy(b, t, k):
        return pltpu.make_async_copy(pk.at[b, pl.ds(t, 1)], xs_hbm.at[pl.ds(idx_ref[0, 0, t * TOP_K + k], 1)],
                                     sem.at[b])

    def drain(b):
        def body(t, carry):
            for k in range(TOP_K):
                row_copy(b, t, k).wait()
            return carry
        lax.fori_loop(0, n_rows // TOP_K, body, 0)

    @pl.when(i == 0)
    def _fill():
        zero[...] = jnp.zeros(zero.shape, zero.dtype)

        def body(j, carry):
            blk = pl.ds(pl.multiple_of(fill_ref[j] * MOE_TM, MOE_TM), MOE_TM)
            cp = pltpu.make_async_copy(zero, xs_hbm.at[blk], zsem.at[0])
            cp.start()
            cp.wait()
            return carry
        lax.fori_loop(0, fill_ref.shape[0], body, 0)

    @pl.when(i >= 2)
    def _reuse():
        drain(buf)

    packed = _pack_bf16_pairs(h1_ref[...])
    pk[buf] = packed
    hp_ref[...] = packed

    def issue(t, carry):
        for k in range(TOP_K):
            row_copy(buf, t, k).start()
        return carry
    lax.fori_loop(0, n_rows // TOP_K, issue, 0)

    @pl.when(i == n - 1)
    def _tail():
        drain(1 - buf)
        drain(buf)


def _dispatch_rows(h1, tok_slots, fill_blocks, n_slots):
    n, d = h1.shape
    tt = SCATTER_TT
    nt = n // tt
    assert nt >= 2
    grid_spec = pltpu.PrefetchScalarGridSpec(
        num_scalar_prefetch=1, grid=(nt,),
        in_specs=[pl.BlockSpec((1, 1, tt * TOP_K), lambda i, f: (i, 0, 0), memory_space=pltpu.SMEM),
                  pl.BlockSpec((tt, d), lambda i, f: (i, 0))],
        out_specs=[pl.BlockSpec((tt, d // 2), lambda i, f: (i, 0)), pl.BlockSpec(memory_space=pl.ANY)],
        scratch_shapes=[pltpu.VMEM((2, tt, d // 2), jnp.uint32), pltpu.VMEM((MOE_TM, d // 2), jnp.uint32),
                        pltpu.SemaphoreType.DMA((2,)), pltpu.SemaphoreType.DMA((1,))])
    return pl.pallas_call(
        _dispatch_rows_body, grid_spec=grid_spec,
        out_shape=[jax.ShapeDtypeStruct((n, d // 2), jnp.uint32),
                   jax.ShapeDtypeStruct((n_slots, d // 2), jnp.uint32)],
        compiler_params=_cparams("arbitrary"), name="moe_dispatch")(
            fill_blocks, tok_slots.reshape(nt, 1, tt * TOP_K), h1)


def _staged_weights(s, flags, fetch, cur, nxt, cast):
    @pl.when(s == 0)
    def _prime():
        for cp in fetch(*cur):
            cp.start()

    @pl.when(flags % 2 == 1)
    def _turn():
        for cp in fetch(*cur):
            cp.wait()
        cast()

        @pl.when(flags >= 4)
        def _ahead():
            for cp in fetch(*nxt):
                cp.start()


def _moe_up_body(rb_ref, ch_ref, e_ref, flags_ref, ne_ref, nch_ref, x_ref, wg_hbm, wu_hbm, hb_ref,
                 wgb_ref, wub_ref, stage, sem):
    s = pl.program_id(0)
    flags = flags_ref[s]

    def fetch(e, ch):
        cols = pl.ds(pl.multiple_of(ch * MOE_TF, MOE_TF), MOE_TF)
        return (pltpu.make_async_copy(wg_hbm.at[e, :, cols], stage.at[0], sem.at[0]),
                pltpu.make_async_copy(wu_hbm.at[e, :, cols], stage.at[1], sem.at[1]))

    def cast():
        wgb_ref[...] = stage[0].astype(BF16)
        wub_ref[...] = stage[1].astype(BF16)

    _staged_weights(s, flags, fetch, (e_ref[s], ch_ref[s]), (ne_ref[s], nch_ref[s]), cast)

    @pl.when((flags // 2) % 2 == 1)
    def _compute():
        lo, hi = _unpack_bf16_pairs(x_ref[...])
        half = lo.shape[1]
        g = _dot(lo, wgb_ref[:half, :]) + _dot(hi, wgb_ref[half:, :])
        u = _dot(lo, wub_ref[:half, :]) + _dot(hi, wub_ref[half:, :])
        hb_ref[...] = (g * _sigmoid(g) * u).astype(hb_ref.dtype)

    @pl.when((flags // 2) % 2 == 0)
    def _blank():
        hb_ref[...] = jnp.zeros(hb_ref.shape, hb_ref.dtype)


def _moe_up(xp, wg, wu, tables):
    rb, chunk, e, flags, next_e, next_chunk = tables
    d = wg.shape[1]
    ff = wg.shape[2]
    grid_spec = pltpu.PrefetchScalarGridSpec(
        num_scalar_prefetch=6, grid=(rb.shape[0],),
        in_specs=[pl.BlockSpec((MOE_TM, d // 2), lambda s, rb, *_: (rb[s], 0)),
                  pl.BlockSpec(memory_space=pl.ANY), pl.BlockSpec(memory_space=pl.ANY)],
        out_specs=pl.BlockSpec((MOE_TM, MOE_TF), lambda s, rb, ch, *_: (rb[s], ch[s])),
        scratch_shapes=[pltpu.VMEM((d, MOE_TF), BF16), pltpu.VMEM((d, MOE_TF), BF16),
                        pltpu.VMEM((2, d, MOE_TF), F32), pltpu.SemaphoreType.DMA((2,))])
    return pl.pallas_call(
        _moe_up_body, grid_spec=grid_spec, out_shape=jax.ShapeDtypeStruct((xp.shape[0], ff), BF16),
        compiler_params=_cparams("arbitrary"), name="moe_up")(rb, chunk, e, flags, next_e, next_chunk, xp, wg, wu)


def _moe_down_body(rb_ref, e_ref, flags_ref, ne_ref, hb_ref, wd_hbm, y_ref, wdb_ref, stage, sem):
    s = pl.program_id(0)
    flags = flags_ref[s]

    def fetch(e):
        return (pltpu.make_async_copy(wd_hbm.at[e], stage, sem.at[0]),)

    def cast():
        wdb_ref[...] = stage[...].astype(BF16)

    _staged_weights(s, flags, fetch, (e_ref[s],), (ne_ref[s],), cast)

    @pl.when((flags // 2) % 2 == 1)
    def _compute():
        y_ref[...] = _dot(hb_ref[...], wdb_ref[...])

    @pl.when((flags // 2) % 2 == 0)
    def _blank():
        y_ref[...] = jnp.zeros(y_ref.shape, y_ref.dtype)


def _moe_down(hb, wd, tables):
    rb, _, e, flags, next_e, _ = tables
    ff = hb.shape[1]
    d = wd.shape[2]
    grid_spec = pltpu.PrefetchScalarGridSpec(
        num_scalar_prefetch=4, grid=(rb.shape[0],),
        in_specs=[pl.BlockSpec((MOE_TM, ff), lambda s, rb, *_: (rb[s], 0)), pl.BlockSpec(memory_space=pl.ANY)],
        out_specs=pl.BlockSpec((MOE_TM, d), lambda s, rb, *_: (rb[s], 0)),
        scratch_shapes=[pltpu.VMEM((ff, d), BF16), pltpu.VMEM((ff, d), F32), pltpu.SemaphoreType.DMA((1,))])
    return pl.pallas_call(
        _moe_down_body, grid_spec=grid_spec, out_shape=jax.ShapeDtypeStruct((hb.shape[0], d), F32),
        compiler_params=_cparams("arbitrary"), name="moe_down")(rb, e, flags, next_e, hb, wd)


def _swiglu_grouped(xp, wg, wu, wd, b_start, nb, nb_max, n_used):
    hb = _moe_up(xp, wg, wu, _step_tables(b_start, nb, nb_max, wg.shape[2] // MOE_TF, n_used))
    return _moe_down(hb, wd, _step_tables(b_start, nb, nb_max, 1, n_used))


def _combine_body(cur_ref, nxt_ref, y_hbm, gate_ref, h1_ref, ysh_ref, g_ref, b_ref, o_ref, gbuf, sem):
    i = pl.program_id(0)
    n = pl.num_programs(0)
    n_rows = cur_ref.shape[2]

    def fetch(idx_ref, buf, t, k):
        return pltpu.make_async_copy(y_hbm.at[pl.ds(idx_ref[0, 0, t * TOP_K + k], 1)],
                                     gbuf.at[buf, k, pl.ds(t, 1)], sem.at[buf])

    def issue(idx_ref, buf):
        def body(t, carry):
            for k in range(TOP_K):
                fetch(idx_ref, buf, t, k).start()
            return carry
        lax.fori_loop(0, n_rows // TOP_K, body, 0)

    @pl.when(i == 0)
    def _first():
        issue(cur_ref, 0)

    @pl.when(i + 1 < n)
    def _ahead():
        issue(nxt_ref, (i + 1) % 2)

    buf = i % 2

    def drain(t, carry):
        for k in range(TOP_K):
            fetch(cur_ref, buf, t, k).wait()
        return carry

    lax.fori_loop(0, n_rows // TOP_K, drain, 0)
    gate = gate_ref[...]
    routed = gbuf[buf, 0] * gate[:, 0:1]
    for k in range(1, TOP_K):
        routed = routed + gbuf[buf, k] * gate[:, k:k + 1]
    o_ref[...] = _layer_norm(ALPHA * h1_ref[...] + (routed + ysh_ref[...]), g_ref[...], b_ref[...])


def _combine_ln2(y, tok_slots, tok_gates, h1, ysh, g, b):
    n, d = h1.shape
    tt = COMBINE_TT
    nt = n // tt
    idx = tok_slots.reshape(nt, 1, tt * TOP_K)
    row = pl.BlockSpec((tt, d), lambda i: (i, 0))
    vec = pl.BlockSpec((1, d), lambda i: (0, 0))
    return pl.pallas_call(
        _combine_body, grid=(nt,),
        in_specs=[pl.BlockSpec((1, 1, tt * TOP_K), lambda i: (i, 0, 0), memory_space=pltpu.SMEM),
                  pl.BlockSpec((1, 1, tt * TOP_K), lambda i: (jnp.minimum(i + 1, nt - 1), 0, 0),
                               memory_space=pltpu.SMEM),
                  pl.BlockSpec(memory_space=pl.ANY), pl.BlockSpec((tt, TOP_K), lambda i: (i, 0)),
                  row, row, vec, vec],
        out_specs=row, out_shape=jax.ShapeDtypeStruct((n, d), F32),
        scratch_shapes=[pltpu.VMEM((2, TOP_K, tt, d), F32), pltpu.SemaphoreType.DMA((2,))],
        compiler_params=_cparams("arbitrary"), name="moe_combine_ln2")(
            idx, idx, y, tok_gates, h1, ysh, g.reshape(1, d), b.reshape(1, d))


def _forward(x_prompt, x_sample, cache_moba_k, cache_moba_v, cache_diff_k, cache_diff_v, page_table,
             ln_in_g, ln_in_b, w_in, diff_lq1, diff_lk1, diff_lq2, diff_lk2, diff_norm_g, w_out,
             ln1_g, ln1_b, w_router, b_router, w_exp_gate, w_exp_up, w_exp_down,
             w_sh_gate, w_sh_up, w_sh_down, ln2_g, ln2_b):
    bsz, s_len, d = x_prompt.shape
    dbsz, t_len, _ = x_sample.shape
    n_p = bsz * s_len
    n_s = dbsz * t_len
    n = n_p + n_s
    mw = d // 2
    m_heads = mw // HEAD_DIM
    d_heads = mw // (2 * HEAD_DIM)
    past = page_table.shape[1] * cache_moba_k.shape[2]
    lyr = 0

    xp, xs = x_prompt.reshape(n_p, d), x_sample.reshape(n_s, d)
    pos = jnp.concatenate([jnp.tile(jnp.arange(s_len, dtype=jnp.int32), bsz),
                           jnp.tile(past + jnp.arange(t_len, dtype=jnp.int32), dbsz)])
    tabs = _rope_tables(pos)
    h = _ln_in(xp, xs, ln_in_g, ln_in_b)
    w_in_b = w_in[lyr].astype(BF16)
    qscale = 1.0 / math.sqrt(HEAD_DIM)
    nat, trn = False, True
    qm, qm_t = _proj(h, w_in_b, 0, mw, ((BF16, nat), (BF16, trn)), tabs, qscale, "proj_qm")
    km, km_b = _proj(h, w_in_b, mw, mw, ((F32, nat), (BF16, nat)), tabs, 1.0, "proj_km")
    vm, vm_t = _proj(h, w_in_b, 2 * mw, mw, ((F32, nat), (BF16, trn)), None, 1.0, "proj_vm")
    qd, qd_t = _proj(h, w_in_b, 3 * mw, mw, ((BF16, nat), (BF16, trn)), tabs, qscale, "proj_qd")
    kd, kd_b = _proj(h, w_in_b, 4 * mw, mw, ((F32, nat), (BF16, nat)), tabs, 1.0, "proj_kd")
    vd, vd_t = _proj(h, w_in_b, 5 * mw, mw, ((F32, nat), (BF16, trn)), None, 1.0, "proj_vd")

    lams = [v[lyr].reshape(1, HEAD_DIM) for v in (diff_lq1, diff_lk1, diff_lq2, diff_lk2)]
    g_diff = diff_norm_g[lyr].reshape(1, 2 * HEAD_DIM)

    nbp = s_len // MOBA_BLOCK
    means = _moba_means(km, n_p).reshape(bsz, nbp, mw)
    means = jnp.pad(means, ((0, 0), (0, (-nbp) % 8), (0, 0)))
    om_p = _moba_prompt(qm_t, km_b, vm_t, means, bsz, s_len)
    od_p = _diff_prompt(qd_t, kd_b, vd_t, g_diff, lams, bsz, s_len)

    def smp(a, w):
        return a[n_p:].reshape(dbsz, t_len * (mw // w), w)

    hw = 2 * HEAD_DIM
    om_s, od_s = _sample_attn(page_table, smp(qm, HEAD_DIM), smp(qd, hw), smp(km, HEAD_DIM), smp(vm, HEAD_DIM),
                              smp(kd, hw), smp(vd, hw), cache_moba_k, cache_moba_v, cache_diff_k,
                              cache_diff_v, g_diff, lams, lyr, t_len)
    o_s = jnp.concatenate([om_s.reshape(n_s, mw), od_s.reshape(n_s, mw)], axis=1)

    o_all = jnp.concatenate([jnp.concatenate([om_p, od_p], axis=1), o_s.astype(BF16)], axis=0)
    (a,) = _proj(o_all, w_out[lyr].astype(BF16), 0, d, ((F32, nat),), None, 1.0, "proj_out")
    h1, tok_exp_t, tok_gate_t, sel_t = _ln1_router(xp, xs, a, ln_in_g, ln_in_b, ln1_g[lyr], ln1_b[lyr],
                                                   w_router[lyr], b_router[lyr])

    tok_slots, fill_blocks, b_start, nb, n_used = _dispatch_tables(sel_t, tok_exp_t)
    nb_max = n * TOP_K // MOE_TM + sel_t.shape[0]
    h1p, xs = _dispatch_rows(h1, tok_slots, fill_blocks, nb_max * MOE_TM)
    y = _swiglu_grouped(xs, w_exp_gate[lyr], w_exp_up[lyr], w_exp_down[lyr], b_start, nb, nb_max, n_used)
    nb_sh = n // MOE_TM
    ysh = _swiglu_grouped(h1p, w_sh_gate[lyr][None], w_sh_up[lyr][None], w_sh_down[lyr][None],
                          jnp.zeros((1,), jnp.int32), jnp.full((1,), nb_sh, jnp.int32), nb_sh, nb_sh)
    tok_gates = tok_gate_t.T
    out = _combine_ln2(y, tok_slots, tok_gates, h1, ysh, ln2_g[lyr], ln2_b[lyr])

    def prm(t, nh):
        return t[:n_p].reshape(1, bsz, s_len, nh, mw // nh)

    def dec(t, nh):
        return t[n_p:].reshape(1, dbsz, t_len, nh, mw // nh)

    outs = (out[:n_p].reshape(bsz, s_len, d), out[n_p:].reshape(dbsz, t_len, d),
            prm(km, m_heads), prm(vm, m_heads), prm(kd, d_heads), prm(vd, d_heads),
            dec(km, m_heads), dec(vm, m_heads), dec(kd, d_heads), dec(vd, d_heads))
    mid = dict(a=a, h1=h1, y=y, tok_slots=tok_slots, tok_gates=tok_gates, ysh=ysh)
    return outs, mid


def kernel(x_prompt, x_sample, cache_moba_k, cache_moba_v, cache_diff_k, cache_diff_v, page_table, ln_in_g, ln_in_b, w_in, diff_lq1, diff_lk1, diff_lq2, diff_lk2, diff_norm_g, w_out, ln1_g, ln1_b, w_router, b_router, w_exp_gate, w_exp_up, w_exp_down, w_sh_gate, w_sh_up, w_sh_down, ln2_g, ln2_b):
    outs, _ = _forward(x_prompt, x_sample, cache_moba_k, cache_moba_v, cache_diff_k, cache_diff_v, page_table,
                       ln_in_g, ln_in_b, w_in, diff_lq1, diff_lk1, diff_lq2, diff_lk2, diff_norm_g, w_out,
                       ln1_g, ln1_b, w_router, b_router, w_exp_gate, w_exp_up, w_exp_down,
                       w_sh_gate, w_sh_up, w_sh_down, ln2_g, ln2_b)
    return outs
```
